```python
import jax, jax.numpy as jnp
from jax import lax
import numpy as np


D_MODEL = 4096
BATCH = 4
SEQ = 2048
DEPTH = 4
DEC_BATCH = 128
DEC_SEQ = 4
PAST_LEN = 16384
PAGE_SIZE = 128

N_BRANCH = 4
D_BR = D_MODEL // 4
CONV_A = 3
POOL_WINDOWS = (2, 4, 8, 16)
N_POOL_GROUPS = len(POOL_WINDOWS)
POOL_GROUP = D_BR // N_POOL_GROUPS
POOL_BUF = max(POOL_WINDOWS) - 1
CHUNK = 128
N_SGU_HEADS = 8
SGU_HEAD = D_BR // N_SGU_HEADS
CONV_D = 31
N_IN = 8 * D_BR + N_BRANCH * D_MODEL
D_FF = ((-(-8 * D_MODEL // 3)) + 255) // 256 * 256
RMS_EPS = 1e-6
LN_EPS = 1e-5

kernel_name = 'hybrid_gated_conv_pool_sgu_conformer_decoder_step'


def rms_norm(x, g):
    xf = x.astype(jnp.float32)
    y = xf * lax.rsqrt(jnp.mean(xf * xf, axis=-1, keepdims=True) + RMS_EPS)
    return (y * g.astype(jnp.float32)).astype(x.dtype)


def layer_norm(x, g, b):
    xf = x.astype(jnp.float32)
    mu = jnp.mean(xf, axis=-1, keepdims=True)
    xc = xf - mu
    var = jnp.mean(xc * xc, axis=-1, keepdims=True)
    y = xc * lax.rsqrt(var + LN_EPS) * g.astype(jnp.float32) + b.astype(jnp.float32)
    return y.astype(x.dtype)


def causal_dwconv(x_pad, w):
    c = x_pad.shape[-1]
    return lax.conv_general_dilated(
        x_pad, w[:, None, :].astype(x_pad.dtype), window_strides=(1,), padding='VALID',
        dimension_numbers=('NWC', 'WIO', 'NWC'), feature_group_count=c)


def multiscale_pool(p_pad, pos0):
    t = p_pad.shape[1] - POOL_BUF
    pf = p_pad.astype(jnp.float32)
    cs = jnp.cumsum(jnp.pad(pf, ((0, 0), (1, 0), (0, 0))), axis=1)
    pos = pos0 + jnp.arange(t, dtype=jnp.int32)
    outs = []
    for gi, w in enumerate(POOL_WINDOWS):
        sl = slice(gi * POOL_GROUP, (gi + 1) * POOL_GROUP)
        s = cs[:, POOL_BUF + 1:, sl] - cs[:, POOL_BUF + 1 - w:POOL_BUF + 1 - w + t, sl]
        cnt = jnp.minimum(w, pos + 1).astype(jnp.float32)
        outs.append(s / cnt[None, :, None] - pf[:, POOL_BUF:, sl])
    return jnp.concatenate(outs, axis=-1).astype(p_pad.dtype)


def spatial_gate(u, v, w_s, b_s):
    bsz, t, _ = v.shape
    n_chunks = -(-t // CHUNK)
    pad = n_chunks * CHUNK - t
    vp = jnp.pad(v, ((0, 0), (0, pad), (0, 0))).reshape(bsz, n_chunks, CHUNK, N_SGU_HEADS, SGU_HEAD)
    mask = jnp.tril(jnp.ones((CHUNK, CHUNK), dtype=bool))
    w = jnp.where(mask[None], w_s, jnp.zeros_like(w_s)).astype(v.dtype)
    s = jnp.einsum('hij,bcjhd->bcihd', w, vp) + b_s.T[:, :, None].astype(v.dtype)
    s = s.reshape(bsz, n_chunks * CHUNK, D_BR)[:, :t]
    return u * s


def token_mixers(h, past_a, past_b, past_d, pos0, w_in, conv_a_w, pool_w, pool_scale,
                 sgu_norm_g, sgu_norm_b, sgu_w, sgu_b, conv_d_w, conv_d_b,
                 conv_norm_g, conv_norm_b, w_branch, w_o):
    bsz, t, _ = h.shape
    z = h @ w_in.astype(h.dtype)
    a_b, a_c, a_h, p_in, c_u, c_v, d_a, d_g, gates = jnp.split(
        z, [D_BR * k for k in range(1, 9)], axis=-1)
    a_pad = jnp.concatenate([past_a, a_c * a_h], axis=1)
    y_a = a_b * causal_dwconv(a_pad, conv_a_w)
    b_pad = jnp.concatenate([past_b, p_in], axis=1)
    pooled = multiscale_pool(b_pad, pos0).reshape(bsz, t, N_POOL_GROUPS, POOL_GROUP)
    y_b = jnp.einsum('btgc,gcd->btgd', pooled, pool_w.astype(h.dtype)).reshape(bsz, t, D_BR) * pool_scale
    u = jax.nn.gelu(c_u, approximate=False)
    v = layer_norm(jax.nn.gelu(c_v, approximate=False), sgu_norm_g, sgu_norm_b)
    y_c = spatial_gate(u, v, sgu_w, sgu_b)
    glu = d_a * jax.nn.sigmoid(d_g)
    d_pad = jnp.concatenate([past_d, glu], axis=1)
    y_d = jax.nn.silu(layer_norm(causal_dwconv(d_pad, conv_d_w) + conv_d_b, conv_norm_g, conv_norm_b))
    branches = jnp.stack([y_a, y_b, y_c, y_d], axis=2)
    proj = jnp.einsum('btnc,ncd->btnd', branches, w_branch.astype(h.dtype))
    g = jax.nn.sigmoid(gates.reshape(bsz, t, N_BRANCH, D_MODEL))
    merged = jnp.sum(g * proj, axis=2)
    out = merged @ w_o.astype(h.dtype)
    return out, a_pad[:, -(CONV_A - 1):], b_pad[:, -POOL_BUF:], d_pad[:, -(CONV_D - 1):], v


def swiglu(h, w_ffn_in, w_ffn_out):
    gu = h @ w_ffn_in.astype(h.dtype)
    gate, up = jnp.split(gu, [D_FF], axis=-1)
    return (jax.nn.silu(gate) * up) @ w_ffn_out.astype(h.dtype)


def decoder_layer(x, past_a, past_b, past_d, pos0, w_in, conv_a_w, pool_w, pool_scale,
                  sgu_norm_g, sgu_norm_b, sgu_w, sgu_b, conv_d_w, conv_d_b, conv_norm_g,
                  conv_norm_b, w_branch, w_o, w_ffn_in, w_ffn_out,
                  norm_mix_pre, norm_mix_post, norm_ffn_pre, norm_ffn_post):
    mix, new_a, new_b, new_d, v = token_mixers(
        rms_norm(x, norm_mix_pre), past_a, past_b, past_d, pos0, w_in, conv_a_w, pool_w,
        pool_scale, sgu_norm_g, sgu_norm_b, sgu_w, sgu_b, conv_d_w, conv_d_b,
        conv_norm_g, conv_norm_b, w_branch, w_o)
    x = x + rms_norm(mix, norm_mix_post)
    x = x + rms_norm(swiglu(rms_norm(x, norm_ffn_pre), w_ffn_in, w_ffn_out), norm_ffn_post)
    return x, new_a, new_b, new_d, v


def setup_inputs(seed: int = 0) -> dict:
    key = jax.random.key(seed)
    ks = jax.random.split(key, 26)
    f32 = jnp.float32

    def nrm(k, shape, scale):
        return jax.random.normal(k, shape, f32) * scale

    def gain(k, shape):
        return 1.0 + 0.02 * jax.random.normal(k, shape, f32)

    return {
        'x_prompt': nrm(ks[0], (BATCH, SEQ, D_MODEL), 1.0),
        'x_sample': nrm(ks[1], (DEC_BATCH, DEC_SEQ, D_MODEL), 1.0),
        'state_conv_a': nrm(ks[2], (DEPTH, DEC_BATCH, CONV_A - 1, D_BR), 1.0),
        'state_pool_b': nrm(ks[3], (DEPTH, DEC_BATCH, POOL_BUF, D_BR), 1.0),
        'state_conv_d': nrm(ks[4], (DEPTH, DEC_BATCH, CONV_D - 1, D_BR), 0.5),
        'w_in': nrm(ks[5], (DEPTH, D_MODEL, N_IN), D_MODEL ** -0.5),
        'conv_a_w': nrm(ks[6], (DEPTH, CONV_A, D_BR), CONV_A ** -0.5),
        'pool_w': nrm(ks[7], (DEPTH, N_POOL_GROUPS, POOL_GROUP, POOL_GROUP), POOL_GROUP ** -0.5),
        'pool_scale': gain(ks[8], (DEPTH, D_BR)),
        'sgu_norm_g': gain(ks[9], (DEPTH, D_BR)),
        'sgu_norm_b': nrm(ks[10], (DEPTH, D_BR), 0.02),
        'sgu_w': nrm(ks[11], (DEPTH, N_SGU_HEADS, CHUNK, CHUNK), CHUNK ** -0.5),
        'sgu_b': gain(ks[12], (DEPTH, N_SGU_HEADS, CHUNK)),
        'conv_d_w': nrm(ks[13], (DEPTH, CONV_D, D_BR), CONV_D ** -0.5),
        'conv_d_b': nrm(ks[14], (DEPTH, D_BR), 0.02),
        'conv_norm_g': gain(ks[15], (DEPTH, D_BR)),
        'conv_norm_b': nrm(ks[16], (DEPTH, D_BR), 0.02),
        'w_branch': nrm(ks[17], (DEPTH, N_BRANCH, D_BR, D_MODEL), D_BR ** -0.5),
        'w_o': nrm(ks[18], (DEPTH, D_MODEL, D_MODEL), D_MODEL ** -0.5),
        'w_ffn_in': nrm(ks[19], (DEPTH, D_MODEL, 2 * D_FF), D_MODEL ** -0.5),
        'w_ffn_out': nrm(ks[20], (DEPTH, D_FF, D_MODEL), D_FF ** -0.5),
        'norm_mix_pre': gain(ks[21], (DEPTH, D_MODEL)),
        'norm_mix_post': gain(ks[22], (DEPTH, D_MODEL)),
        'norm_ffn_pre': gain(ks[23], (DEPTH, D_MODEL)),
        'norm_ffn_post': gain(ks[24], (DEPTH, D_MODEL)),
    }


def reference(x_prompt, x_sample, state_conv_a, state_pool_b, state_conv_d,
              w_in, conv_a_w, pool_w, pool_scale, sgu_norm_g, sgu_norm_b, sgu_w, sgu_b,
              conv_d_w, conv_d_b, conv_norm_g, conv_norm_b, w_branch, w_o,
              w_ffn_in, w_ffn_out, norm_mix_pre, norm_mix_post, norm_ffn_pre, norm_ffn_post):
    xp, xs = x_prompt, x_sample
    bp = xp.shape[0]
    na_p, na_s, nb_p, nb_s, nd_p, nd_s, nv_s = [], [], [], [], [], [], []
    for l in range(DEPTH):
        params = (w_in[l], conv_a_w[l], pool_w[l], pool_scale[l], sgu_norm_g[l], sgu_norm_b[l],
                  sgu_w[l], sgu_b[l], conv_d_w[l], conv_d_b[l], conv_norm_g[l], conv_norm_b[l],
                  w_branch[l], w_o[l], w_ffn_in[l], w_ffn_out[l], norm_mix_pre[l],
                  norm_mix_post[l], norm_ffn_pre[l], norm_ffn_post[l])
        zero_a = jnp.zeros((bp, CONV_A - 1, D_BR), xp.dtype)
        zero_b = jnp.zeros((bp, POOL_BUF, D_BR), xp.dtype)
        zero_d = jnp.zeros((bp, CONV_D - 1, D_BR), xp.dtype)
        xp, a_p, b_p, d_p, _ = decoder_layer(xp, zero_a, zero_b, zero_d, 0, *params)
        xs, a_s, b_s, d_s, v_s = decoder_layer(
            xs, state_conv_a[l].astype(xs.dtype), state_pool_b[l].astype(xs.dtype),
            state_conv_d[l].astype(xs.dtype), PAST_LEN, *params)
        na_p.append(a_p); na_s.append(a_s)
        nb_p.append(b_p); nb_s.append(b_s)
        nd_p.append(d_p); nd_s.append(d_s)
        nv_s.append(v_s)
    return (xp, xs, jnp.stack(na_p), jnp.stack(na_s), jnp.stack(nb_p), jnp.stack(nb_s),
            jnp.stack(nd_p), jnp.stack(nd_s), jnp.stack(nv_s))
```

```python
import functools
import math

import jax
import jax.numpy as jnp
from jax import lax
from jax.experimental import pallas as pl
from jax.experimental.pallas import tpu as pltpu

D_MODEL = 4096
BATCH = 4
SEQ = 2048
DEPTH = 4
DEC_BATCH = 128
DEC_SEQ = 4
PAST_LEN = 16384
D_BR = D_MODEL // 4
CONV_A = 3
POOL_WINDOWS = (2, 4, 8, 16)
POOL_GROUP = D_BR // len(POOL_WINDOWS)
POOL_BUF = max(POOL_WINDOWS) - 1
CHUNK = 128
N_SGU_HEADS = 8
SGU_HEAD = D_BR // N_SGU_HEADS
CONV_D = 31
N_MIX = 8 * D_BR
N_IN = N_MIX + 4 * D_MODEL
D_FF = 11008
RMS_EPS = 1e-6
LN_EPS = 1e-5

M_PROMPT = BATCH * SEQ
M_SAMPLE = DEC_BATCH * DEC_SEQ
M_ROWS = M_PROMPT + M_SAMPLE

LANES = 128
VMEM_LIMIT_CAP = 58 * 1024 * 1024

F32 = jnp.float32
BF16 = jnp.bfloat16


def _params(n_grid, vmem_bytes):
    limit = min(int(vmem_bytes * 1.15) + (4 << 20), VMEM_LIMIT_CAP)
    return pltpu.CompilerParams(
        dimension_semantics=("arbitrary",) * n_grid, vmem_limit_bytes=limit)


def _grid_spec(grid, in_specs, out_specs, scratch_shapes=()):
    return pltpu.PrefetchScalarGridSpec(
        num_scalar_prefetch=1, grid=grid, in_specs=in_specs,
        out_specs=out_specs, scratch_shapes=list(scratch_shapes))


NORM_ROWS = 272


def _rms(x, g):
    return x * lax.rsqrt(jnp.mean(x * x, axis=-1, keepdims=True) + RMS_EPS) * g


def _norm_kernel(l_ref, x_ref, g_ref, h_ref):
    h_ref[...] = _rms(x_ref[...], g_ref[...]).astype(BF16)


def _first_norm(lidx, x, g_pre):
    row = pl.BlockSpec((NORM_ROWS, D_MODEL), lambda i, l: (i, 0))
    gain = pl.BlockSpec((None, 1, D_MODEL), lambda i, l: (l[0], 0, 0))
    return pl.pallas_call(
        _norm_kernel,
        grid_spec=_grid_spec((M_ROWS // NORM_ROWS,), [row, gain], row),
        out_shape=jax.ShapeDtypeStruct((M_ROWS, D_MODEL), BF16),
        compiler_params=_params(1, NORM_ROWS * D_MODEL * 12),
        name="first_norm",
    )(lidx, x, g_pre)


def _resid_norm_kernel(l_ref, x_ref, o_ref, gpost_ref, gnext_ref, xn_ref, h_ref):
    xn = x_ref[...] + _rms(o_ref[...], gpost_ref[...])
    xn_ref[...] = xn
    h_ref[...] = _rms(xn, gnext_ref[...]).astype(BF16)


def _resid_kernel(l_ref, x_ref, o_ref, gpost_ref, xn_ref):
    xn_ref[...] = x_ref[...] + _rms(o_ref[...], gpost_ref[...])


def _resid_norm(lidx, x, o, g_post, g_next, next_shift):
    row = pl.BlockSpec((NORM_ROWS, D_MODEL), lambda i, l: (i, 0))
    gpost = pl.BlockSpec((None, 1, D_MODEL), lambda i, l: (l[0], 0, 0))
    gnext = pl.BlockSpec((None, 1, D_MODEL), lambda i, l: (l[0] + next_shift, 0, 0))
    return pl.pallas_call(
        _resid_norm_kernel,
        grid_spec=_grid_spec((M_ROWS // NORM_ROWS,), [row, row, gpost, gnext], [row, row]),
        out_shape=[jax.ShapeDtypeStruct((M_ROWS, D_MODEL), F32),
                   jax.ShapeDtypeStruct((M_ROWS, D_MODEL), BF16)],
        compiler_params=_params(1, NORM_ROWS * D_MODEL * 28),
        name="resid_norm",
    )(lidx, x, o, g_post, g_next)


def _resid(lidx, x, o, g_post):
    row = pl.BlockSpec((NORM_ROWS, D_MODEL), lambda i, l: (i, 0))
    gpost = pl.BlockSpec((None, 1, D_MODEL), lambda i, l: (l[0], 0, 0))
    return pl.pallas_call(
        _resid_kernel,
        grid_spec=_grid_spec((M_ROWS // NORM_ROWS,), [row, row, gpost], row),
        out_shape=jax.ShapeDtypeStruct((M_ROWS, D_MODEL), F32),
        compiler_params=_params(1, NORM_ROWS * D_MODEL * 24),
        name="resid",
    )(lidx, x, o, g_post)


def _mm_kernel(l_ref, x_ref, w_ref, o_ref, wb_ref):
    @pl.when(pl.program_id(1) == 0)
    def _():
        wb_ref[...] = w_ref[...].astype(BF16)

    o_ref[...] = jnp.dot(x_ref[...], wb_ref[...],
                         preferred_element_type=F32).astype(o_ref.dtype)


def _mm_acc_kernel(l_ref, x_ref, w_ref, acc_ref, o_ref, wb_ref):
    @pl.when(pl.program_id(1) == 0)
    def _():
        wb_ref[...] = w_ref[...].astype(BF16)

    o_ref[...] = acc_ref[...] + jnp.dot(x_ref[...], wb_ref[...],
                                        preferred_element_type=F32)


def _matmul(lidx, x, w, *, bm, bn, k_block=0, k_blocks=1, acc=None, name):
    m = x.shape[0]
    kt = x.shape[1] // k_blocks
    n = w.shape[2]
    x_spec = pl.BlockSpec((bm, kt), lambda j, i, l: (i, k_block))
    w_spec = pl.BlockSpec((None, kt, bn), lambda j, i, l: (l[0], k_block, j))
    o_spec = pl.BlockSpec((bm, bn), lambda j, i, l: (i, j))
    vmem = 2 * bm * kt * 2 + kt * bn * 10 + 2 * bm * bn * 4
    if acc is None:
        kern, in_specs, args, aliases = _mm_kernel, [x_spec, w_spec], (lidx, x, w), {}
    else:
        kern, in_specs, args = _mm_acc_kernel, [x_spec, w_spec, o_spec], (lidx, x, w, acc)
        aliases = {3: 0}
        vmem += 2 * bm * bn * 4
    return pl.pallas_call(
        kern,
        grid_spec=_grid_spec((n // bn, m // bm), in_specs, o_spec,
                             [pltpu.VMEM((kt, bn), BF16)]),
        out_shape=jax.ShapeDtypeStruct((m, n), F32),
        input_output_aliases=aliases,
        compiler_params=_params(2, vmem),
        name=name,
    )(*args)


FFN_BN = 256
FFN_BM = 1088


def _ffn_in_kernel(l_ref, x_ref, wg_ref, wu_ref, o_ref, wb_ref):
    @pl.when(pl.program_id(1) == 0)
    def _():
        wb_ref[:, :FFN_BN] = wg_ref[...].astype(BF16)
        wb_ref[:, FFN_BN:] = wu_ref[...].astype(BF16)

    gu = jnp.dot(x_ref[...], wb_ref[...], preferred_element_type=F32)
    gate = gu[:, :FFN_BN]
    o_ref[...] = (gate * jax.nn.sigmoid(gate) * gu[:, FFN_BN:]).astype(BF16)


def _ffn_in(lidx, h, w_ffn_in):
    nb = D_FF // FFN_BN
    x_spec = pl.BlockSpec((FFN_BM, D_MODEL), lambda j, i, l: (i, 0))
    wg_spec = pl.BlockSpec((None, D_MODEL, FFN_BN), lambda j, i, l: (l[0], 0, j))
    wu_spec = pl.BlockSpec((None, D_MODEL, FFN_BN), lambda j, i, l: (l[0], 0, j + nb))
    o_spec = pl.BlockSpec((FFN_BM, FFN_BN), lambda j, i, l: (i, j))
    vmem = 2 * FFN_BM * D_MODEL * 2 + D_MODEL * FFN_BN * 20 + 2 * FFN_BM * FFN_BN * 2
    return pl.pallas_call(
        _ffn_in_kernel,
        grid_spec=_grid_spec((nb, M_ROWS // FFN_BM), [x_spec, wg_spec, wu_spec], o_spec,
                             [pltpu.VMEM((D_MODEL, 2 * FFN_BN), BF16)]),
        out_shape=jax.ShapeDtypeStruct((M_ROWS, D_FF), BF16),
        compiler_params=_params(2, vmem),
        name="ffn_in",
    )(lidx, h, w_ffn_in, w_ffn_in)


BR_BM = 544
BR_BN = 512


def _branch_kernel(l_ref, y_ref, g0_ref, g1_ref, g2_ref, g3_ref, p_ref, o_ref, pb_ref):
    @pl.when(pl.program_id(1) == 0)
    def _():
        pb_ref[...] = p_ref[...].astype(BF16)

    acc = None
    for n, g_ref in enumerate((g0_ref, g1_ref, g2_ref, g3_ref)):
        proj = jnp.dot(y_ref[:, n * D_BR:(n + 1) * D_BR], pb_ref[n],
                       preferred_element_type=F32)
        term = jax.nn.sigmoid(g_ref[...]) * proj
        acc = term if acc is None else acc + term
    o_ref[...] = acc.astype(BF16)


def _branch_merge(lidx, y, z, w_branch):
    gate0 = N_MIX // BR_BN
    per_gate = D_MODEL // BR_BN
    y_spec = pl.BlockSpec((BR_BM, D_MODEL), lambda j, i, l: (i, 0))
    g_specs = [pl.BlockSpec((BR_BM, BR_BN),
                            functools.partial(lambda j, i, l, n: (i, gate0 + n * per_gate + j), n=n))
               for n in range(4)]
    p_spec = pl.BlockSpec((None, 4, D_BR, BR_BN), lambda j, i, l: (l[0], 0, 0, j))
    o_spec = pl.BlockSpec((BR_BM, BR_BN), lambda j, i, l: (i, j))
    vmem = (2 * BR_BM * D_MODEL * 2 + 8 * BR_BM * BR_BN * 4 + 4 * D_BR * BR_BN * 10
            + 2 * BR_BM * BR_BN * 2 + 4 * BR_BM * BR_BN * 4)
    return pl.pallas_call(
        _branch_kernel,
        grid_spec=_grid_spec((D_MODEL // BR_BN, M_ROWS // BR_BM),
                             [y_spec] + g_specs + [p_spec], o_spec,
                             [pltpu.VMEM((4, D_BR, BR_BN), BF16)]),
        out_shape=jax.ShapeDtypeStruct((M_ROWS, D_MODEL), BF16),
        compiler_params=_params(2, vmem),
        name="branch_merge",
    )(lidx, y, z, z, z, z, w_branch)


def _gelu(x):
    return 0.5 * x * (1.0 + lax.erf(x * (1.0 / math.sqrt(2.0))))


def _layer_norm(x, g, b):
    mu = jnp.mean(x, axis=-1, keepdims=True)
    xc = x - mu
    var = jnp.mean(xc * xc, axis=-1, keepdims=True)
    return xc * lax.rsqrt(var + LN_EPS) * g + b


def _silu(x):
    return x * jax.nn.sigmoid(x)


MIX_TB = 256
HALO_A, HALO_B, HALO_D = 8, 16, 32


def _mix_prompt_kernel(l_ref, ab_ref, ac_ref, ah_ref, p_ref, cu_ref, cv_ref, da_ref, dg_ref,
                       ach_ref, ahh_ref, ph_ref, dah_ref, dgh_ref,
                       caw_ref, pw_ref, ps_ref, sng_ref, snb_ref, sw_ref, sbt_ref,
                       cdw_ref, cdb_ref, cng_ref, cnb_ref,
                       y_ref, na_ref, nb_ref, nd_ref,
                       qpad, ppad, dpad, conv_scr):
    t = pl.program_id(1)
    has_past = t > 0
    tb = MIX_TB

    qpad[0:HALO_A, :] = jnp.where(has_past, ach_ref[...] * ahh_ref[...], 0.0)
    qpad[HALO_A:, :] = ac_ref[...] * ah_ref[...]
    conv_a = (caw_ref[0:1, :] * qpad[HALO_A - 2:HALO_A - 2 + tb, :]
              + caw_ref[1:2, :] * qpad[HALO_A - 1:HALO_A - 1 + tb, :]
              + caw_ref[2:3, :] * qpad[HALO_A:HALO_A + tb, :])
    y_ref[:, 0:D_BR] = (ab_ref[...] * conv_a).astype(BF16)

    ppad[0:HALO_B, :] = jnp.where(has_past, ph_ref[...], 0.0)
    ppad[HALO_B:, :] = p_ref[...]
    pos = t * tb + lax.broadcasted_iota(jnp.int32, (tb, 1), 0)
    for gi, w in enumerate(POOL_WINDOWS):
        cols = slice(gi * POOL_GROUP, (gi + 1) * POOL_GROUP)
        s = ppad[HALO_B:HALO_B + tb, cols]
        for i in range(1, w):
            s = s + ppad[HALO_B - i:HALO_B - i + tb, cols]
        cnt = jnp.minimum(w, pos + 1).astype(F32)
        pooled = s / cnt - ppad[HALO_B:HALO_B + tb, cols]
        yb = jnp.dot(pooled.astype(BF16), pw_ref[gi].astype(BF16), preferred_element_type=F32)
        y_ref[:, D_BR + gi * POOL_GROUP:D_BR + (gi + 1) * POOL_GROUP] = (
            yb * ps_ref[:, cols]).astype(BF16)

    u = _gelu(cu_ref[...])
    v = _layer_norm(_gelu(cv_ref[...]), sng_ref[...], snb_ref[...]).astype(BF16)
    tril = (lax.broadcasted_iota(jnp.int32, (CHUNK, CHUNK), 0)
            >= lax.broadcasted_iota(jnp.int32, (CHUNK, CHUNK), 1))
    for h in range(N_SGU_HEADS):
        w_h = jnp.where(tril, sw_ref[h], 0.0).astype(BF16)
        bias = sbt_ref[:, h:h + 1]
        hc = slice(h * SGU_HEAD, (h + 1) * SGU_HEAD)
        for c in range(tb // CHUNK):
            rows = slice(c * CHUNK, (c + 1) * CHUNK)
            s = jnp.dot(w_h, v[rows, hc], preferred_element_type=F32) + bias
            y_ref[rows, 2 * D_BR + h * SGU_HEAD:2 * D_BR + (h + 1) * SGU_HEAD] = (
                u[rows, hc] * s).astype(BF16)

    dpad[0:HALO_D, :] = jnp.where(has_past, dah_ref[...] * jax.nn.sigmoid(dgh_ref[...]), 0.0)
    dpad[HALO_D:, :] = da_ref[...] * jax.nn.sigmoid(dg_ref[...])
    base = HALO_D - (CONV_D - 1)
    for c in range(D_BR // LANES):
        cols = slice(c * LANES, (c + 1) * LANES)
        acc = cdw_ref[0:1, cols] * dpad[base:base + tb, cols]
        for k in range(1, CONV_D):
            acc = acc + cdw_ref[k:k + 1, cols] * dpad[base + k:base + k + tb, cols]
        conv_scr[:, cols] = acc
    conv_d = conv_scr[...] + cdb_ref[...]
    y_ref[:, 3 * D_BR:] = _silu(_layer_norm(conv_d, cng_ref[...], cnb_ref[...])).astype(BF16)

    @pl.when(t == pl.num_programs(1) - 1)
    def _():
        na_ref[...] = qpad[HALO_A + tb - (CONV_A - 1):HALO_A + tb, :]
        nb_ref[...] = ppad[HALO_B + tb - POOL_BUF:HALO_B + tb, :]
        nd_ref[...] = dpad[HALO_D + tb - (CONV_D - 1):HALO_D + tb, :]


def _layer_param_specs(index_args):
    def spec(shape):
        zeros = (0,) * len(shape)
        if index_args == 2:
            return pl.BlockSpec((None,) + shape, lambda b, t, l: (l[0],) + zeros)
        return pl.BlockSpec((None,) + shape, lambda s, l: (l[0],) + zeros)
    return spec


def _mix_prompt(lidx, z, prm):
    tb = MIX_TB
    n_t = SEQ // tb

    def cur(c):
        return pl.BlockSpec((tb, D_BR), lambda b, t, l: (b * n_t + t, c))

    def halo(c, rows):
        per = tb // rows
        return pl.BlockSpec(
            (rows, D_BR), lambda b, t, l: (jnp.maximum((b * n_t + t) * per - 1, 0), c))

    spec = _layer_param_specs(2)
    in_specs = ([cur(c) for c in range(8)]
                + [halo(1, HALO_A), halo(2, HALO_A), halo(3, HALO_B),
                   halo(6, HALO_D), halo(7, HALO_D)]
                + [spec((CONV_A, D_BR)), spec((4, POOL_GROUP, POOL_GROUP)), spec((1, D_BR)),
                   spec((1, D_BR)), spec((1, D_BR)), spec((N_SGU_HEADS, CHUNK, CHUNK)),
                   spec((CHUNK, N_SGU_HEADS)), spec((CONV_D, D_BR)), spec((1, D_BR)),
                   spec((1, D_BR)), spec((1, D_BR))])
    out_specs = [pl.BlockSpec((tb, D_MODEL), lambda b, t, l: (b * n_t + t, 0)),
                 pl.BlockSpec((None, CONV_A - 1, D_BR), lambda b, t, l: (b, 0, 0)),
                 pl.BlockSpec((None, POOL_BUF, D_BR), lambda b, t, l: (b, 0, 0)),
                 pl.BlockSpec((None, CONV_D - 1, D_BR), lambda b, t, l: (b, 0, 0))]
    out_shape = [jax.ShapeDtypeStruct((M_ROWS, D_MODEL), BF16),
                 jax.ShapeDtypeStruct((BATCH, CONV_A - 1, D_BR), F32),
                 jax.ShapeDtypeStruct((BATCH, POOL_BUF, D_BR), F32),
                 jax.ShapeDtypeStruct((BATCH, CONV_D - 1, D_BR), F32)]
    scratch = [pltpu.VMEM((HALO_A + tb, D_BR), F32), pltpu.VMEM((HALO_B + tb, D_BR), F32),
               pltpu.VMEM((HALO_D + tb, D_BR), F32), pltpu.VMEM((tb, D_BR), F32)]
    vmem = 2 * 8 * tb * D_BR * 4 + 2 * tb * D_MODEL * 2 + 5 * tb * D_BR * 4 + 24 * tb * D_BR * 4
    return pl.pallas_call(
        _mix_prompt_kernel,
        grid_spec=_grid_spec((BATCH, n_t), in_specs, out_specs, scratch),
        out_shape=out_shape,
        compiler_params=_params(2, vmem),
        name="mix_prompt",
    )(lidx, *([z] * 13), prm["conv_a_w"], prm["pool_w"], prm["pool_scale"], prm["sgu_norm_g"],
      prm["sgu_norm_b"], prm["sgu_w"], prm["sgu_b_t"], prm["conv_d_w"], prm["conv_d_b"],
      prm["conv_norm_g"], prm["conv_norm_b"])


MIX_SS = 16


def _mix_sample_kernel(l_ref, ab_ref, ac_ref, ah_ref, p_ref, cu_ref, cv_ref, da_ref, dg_ref,
                       pa_ref, pb_ref, pd_ref,
                       caw_ref, pw_ref, ps_ref, sng_ref, snb_ref, swc_ref, sbc_ref,
                       cdw_ref, cdb_ref, cng_ref, cnb_ref,
                       y_ref, na_ref, nb_ref, nd_ref, nv_ref):
    ss = MIX_SS

    a_rows = ([pa_ref[r] for r in range(CONV_A - 1)]
              + [ac_ref[t] * ah_ref[t] for t in range(DEC_SEQ)])
    for t in range(DEC_SEQ):
        conv_a = (caw_ref[0:1, :] * a_rows[t] + caw_ref[1:2, :] * a_rows[t + 1]
                  + caw_ref[2:3, :] * a_rows[t + 2])
        y_ref[t, :, 0:D_BR] = (ab_ref[t] * conv_a).astype(BF16)
    for r in range(CONV_A - 1):
        na_ref[r] = a_rows[DEC_SEQ + r]

    def b_row(r, cols):
        if r < POOL_BUF:
            return pb_ref[r, :, cols]
        return p_ref[r - POOL_BUF, :, cols]

    for gi, w in enumerate(POOL_WINDOWS):
        cols = slice(gi * POOL_GROUP, (gi + 1) * POOL_GROUP)
        pooled = []
        for t in range(DEC_SEQ):
            s = b_row(POOL_BUF + t, cols)
            for i in range(1, w):
                s = s + b_row(POOL_BUF + t - i, cols)
            pooled.append(s / float(w) - b_row(POOL_BUF + t, cols))
        yb = jnp.dot(jnp.concatenate(pooled, axis=0).astype(BF16), pw_ref[gi].astype(BF16),
                     preferred_element_type=F32) * ps_ref[:, cols]
        for t in range(DEC_SEQ):
            y_ref[t, :, D_BR + gi * POOL_GROUP:D_BR + (gi + 1) * POOL_GROUP] = (
                yb[t * ss:(t + 1) * ss].astype(BF16))
    for r in range(POOL_BUF - DEC_SEQ):
        nb_ref[r] = pb_ref[r + DEC_SEQ]
    for t in range(DEC_SEQ):
        nb_ref[POOL_BUF - DEC_SEQ + t] = p_ref[t]

    v_rows = [_layer_norm(_gelu(cv_ref[t]), sng_ref[...], snb_ref[...]) for t in range(DEC_SEQ)]
    for t in range(DEC_SEQ):
        nv_ref[t] = v_rows[t]
        s = sbc_ref[t:t + 1, :]
        for j in range(t + 1):
            s = s + swc_ref[t * DEC_SEQ + j:t * DEC_SEQ + j + 1, :] * v_rows[j]
        y_ref[t, :, 2 * D_BR:3 * D_BR] = (_gelu(cu_ref[t]) * s).astype(BF16)

    hist = CONV_D - 1
    for r in range(hist - DEC_SEQ):
        nd_ref[r] = pd_ref[r + DEC_SEQ]
    for t in range(DEC_SEQ):
        nd_ref[hist - DEC_SEQ + t] = da_ref[t] * jax.nn.sigmoid(dg_ref[t])

    def d_row(r, cols):
        if r < hist:
            return pd_ref[r, :, cols]
        return nd_ref[r - DEC_SEQ, :, cols]

    for t in range(DEC_SEQ):
        parts = []
        for c in range(D_BR // LANES):
            cols = slice(c * LANES, (c + 1) * LANES)
            acc = cdb_ref[:, cols] + cdw_ref[0:1, cols] * d_row(t, cols)
            for k in range(1, CONV_D):
                acc = acc + cdw_ref[k:k + 1, cols] * d_row(t + k, cols)
            parts.append(acc)
        conv_d = jnp.concatenate(parts, axis=1)
        y_ref[t, :, 3 * D_BR:] = _silu(
            _layer_norm(conv_d, cng_ref[...], cnb_ref[...])).astype(BF16)


def _mix_sample(lidx, zs, states, prm):
    def cur(c):
        return pl.BlockSpec((DEC_SEQ, MIX_SS, D_BR), lambda s, l: (0, s, c))

    def state(rows):
        return pl.BlockSpec((None, rows, MIX_SS, D_BR), lambda s, l: (l[0], 0, s, 0))

    def new_state(rows):
        return pl.BlockSpec((rows, MIX_SS, D_BR), lambda s, l: (0, s, 0))

    spec = _layer_param_specs(1)
    in_specs = ([cur(c) for c in range(8)]
                + [state(CONV_A - 1), state(POOL_BUF), state(CONV_D - 1)]
                + [spec((CONV_A, D_BR)), spec((4, POOL_GROUP, POOL_GROUP)), spec((1, D_BR)),
                   spec((1, D_BR)), spec((1, D_BR)), spec((DEC_SEQ * DEC_SEQ, D_BR)),
                   spec((DEC_SEQ, D_BR)), spec((CONV_D, D_BR)), spec((1, D_BR)),
                   spec((1, D_BR)), spec((1, D_BR))])
    out_specs = [pl.BlockSpec((DEC_SEQ, MIX_SS, D_MODEL), lambda s, l: (0, s, 0)),
                 new_state(CONV_A - 1), new_state(POOL_BUF), new_state(CONV_D - 1),
                 new_state(DEC_SEQ)]
    out_shape = [jax.ShapeDtypeStruct((DEC_SEQ, DEC_BATCH, D_MODEL), BF16),
                 jax.ShapeDtypeStruct((CONV_A - 1, DEC_BATCH, D_BR), F32),
                 jax.ShapeDtypeStruct((POOL_BUF, DEC_BATCH, D_BR), F32),
                 jax.ShapeDtypeStruct((CONV_D - 1, DEC_BATCH, D_BR), F32),
                 jax.ShapeDtypeStruct((DEC_SEQ, DEC_BATCH, D_BR), F32)]
    state_rows = (CONV_A - 1) + POOL_BUF + (CONV_D - 1) + DEC_SEQ
    tile = MIX_SS * D_BR * 4
    vmem = (2 * 8 * DEC_SEQ * tile + 4 * state_rows * tile + 2 * DEC_SEQ * MIX_SS * D_MODEL * 2
            + 64 * tile)
    return pl.pallas_call(
        _mix_sample_kernel,
        grid_spec=_grid_spec((DEC_BATCH // MIX_SS,), in_specs, out_specs),
        out_shape=out_shape,
        compiler_params=_params(1, vmem),
        name="mix_sample",
    )(lidx, *([zs] * 8), states["a"], states["b"], states["d"],
      prm["conv_a_w"], prm["pool_w"], prm["pool_scale"], prm["sgu_norm_g"], prm["sgu_norm_b"],
      prm["sgu_w_coef"], prm["sgu_b_coef"], prm["conv_d_w"], prm["conv_d_b"],
      prm["conv_norm_g"], prm["conv_norm_b"])


def _row_param(p):
    return p.reshape(DEPTH, 1, p.shape[-1])


def kernel(x_prompt, x_sample, state_conv_a, state_pool_b, state_conv_d, w_in, conv_a_w, pool_w, pool_scale, sgu_norm_g, sgu_norm_b, sgu_w, sgu_b, conv_d_w, conv_d_b, conv_norm_g, conv_norm_b, w_branch, w_o, w_ffn_in, w_ffn_out, norm_mix_pre, norm_mix_post, norm_ffn_pre, norm_ffn_post):
    x = jnp.concatenate([x_prompt.reshape(M_PROMPT, D_MODEL),
                         x_sample.reshape(M_SAMPLE, D_MODEL)], axis=0)

    sw = sgu_w[:, :, :DEC_SEQ, :DEC_SEQ]
    sgu_w_coef = jnp.repeat(jnp.transpose(sw, (0, 2, 3, 1)), SGU_HEAD, axis=-1)
    sgu_w_coef = sgu_w_coef.reshape(DEPTH, DEC_SEQ * DEC_SEQ, D_BR)
    sgu_b_coef = jnp.repeat(jnp.transpose(sgu_b[:, :, :DEC_SEQ], (0, 2, 1)), SGU_HEAD, axis=-1)

    prm = {
        "conv_a_w": conv_a_w, "pool_w": pool_w, "pool_scale": _row_param(pool_scale),
        "sgu_norm_g": _row_param(sgu_norm_g), "sgu_norm_b": _row_param(sgu_norm_b),
        "sgu_w": sgu_w, "sgu_b_t": jnp.transpose(sgu_b, (0, 2, 1)),
        "sgu_w_coef": sgu_w_coef, "sgu_b_coef": sgu_b_coef,
        "conv_d_w": conv_d_w, "conv_d_b": _row_param(conv_d_b),
        "conv_norm_g": _row_param(conv_norm_g), "conv_norm_b": _row_param(conv_norm_b),
    }
    states = {"a": jnp.transpose(state_conv_a, (0, 2, 1, 3)),
              "b": jnp.transpose(state_pool_b, (0, 2, 1, 3)),
              "d": jnp.transpose(state_conv_d, (0, 2, 1, 3))}
    g_mix_pre, g_mix_post = _row_param(norm_mix_pre), _row_param(norm_mix_post)
    g_ffn_pre, g_ffn_post = _row_param(norm_ffn_pre), _row_param(norm_ffn_post)

    outs = {k: [] for k in ("a_p", "a_s", "b_p", "b_s", "d_p", "d_s", "v_s")}
    h = None
    for layer in range(DEPTH):
        lidx = jnp.full((1,), layer, jnp.int32)
        if layer == 0:
            h = _first_norm(lidx, x, g_mix_pre)
        z = _matmul(lidx, h, w_in, bm=1088, bn=512, name="proj_in")
        y, a_p, b_p, d_p = _mix_prompt(lidx, z, prm)
        zs = jnp.transpose(z[M_PROMPT:, :N_MIX].reshape(DEC_BATCH, DEC_SEQ, N_MIX), (1, 0, 2))
        ys, a_s, b_s, d_s, v_s = _mix_sample(lidx, zs, states, prm)
        y = lax.dynamic_update_slice(
            y, jnp.transpose(ys, (1, 0, 2)).reshape(M_SAMPLE, D_MODEL), (M_PROMPT, 0))
        merged = _branch_merge(lidx, y, z, w_branch)
        mix = _matmul(lidx, merged, w_o, bm=1088, bn=512, name="proj_o")
        x, h = _resid_norm(lidx, x, mix, g_mix_post, g_ffn_pre, 0)
        act = _ffn_in(lidx, h, w_ffn_in)
        ffn = _matmul(lidx, act, w_ffn_out, bm=544, bn=512, k_block=0, k_blocks=2,
                      name="ffn_out_lo")
        ffn = _matmul(lidx, act, w_ffn_out, bm=544, bn=512, k_block=1, k_blocks=2, acc=ffn,
                      name="ffn_out_hi")
        if layer + 1 < DEPTH:
            x, h = _resid_norm(lidx, x, ffn, g_ffn_post, g_mix_pre, 1)
        else:
            x = _resid(lidx, x, ffn, g_ffn_post)
        a_s, b_s, d_s, v_s = (jnp.transpose(s, (1, 0, 2)) for s in (a_s, b_s, d_s, v_s))
        for k, val in zip(outs, (a_p, a_s, b_p, b_s, d_p, d_s, v_s)):
            outs[k].append(val)

    y_prompt = x[:M_PROMPT].reshape(BATCH, SEQ, D_MODEL)
    y_sample = x[M_PROMPT:].reshape(DEC_BATCH, DEC_SEQ, D_MODEL)
    return (y_prompt, y_sample) + tuple(jnp.stack(outs[k]) for k in outs)
```

```python
import math

import jax
import jax.numpy as jnp
from jax import lax
from jax.experimental import pallas as pl
from jax.experimental.pallas import tpu as pltpu

D_MODEL = 4096
BATCH = 4
SEQ = 2048
DEPTH = 4
DEC_BATCH = 128
DEC_SEQ = 4
PAST_LEN = 16384
D_BR = D_MODEL // 4
CONV_A = 3
POOL_WINDOWS = (2, 4, 8, 16)
POOL_GROUP = D_BR // len(POOL_WINDOWS)
POOL_BUF = max(POOL_WINDOWS) - 1
CHUNK = 128
N_SGU_HEADS = 8
SGU_HEAD = D_BR // N_SGU_HEADS
CONV_D = 31
N_MIX = 8 * D_BR
N_IN = N_MIX + 4 * D_MODEL
D_FF = 11008
RMS_EPS = 1e-6
LN_EPS = 1e-5

M_PROMPT = BATCH * SEQ
M_SAMPLE = DEC_BATCH * DEC_SEQ
M_ROWS = M_PROMPT + M_SAMPLE

LANES = 128
VMEM_LIMIT_CAP = 58 * 1024 * 1024

F32 = jnp.float32
BF16 = jnp.bfloat16


def _params(n_grid, vmem_bytes):
    limit = min(int(vmem_bytes * 1.15) + (4 << 20), VMEM_LIMIT_CAP)
    return pltpu.CompilerParams(
        dimension_semantics=("arbitrary",) * n_grid, vmem_limit_bytes=limit)


def _grid_spec(grid, in_specs, out_specs, scratch_shapes=()):
    return pltpu.PrefetchScalarGridSpec(
        num_scalar_prefetch=1, grid=grid, in_specs=in_specs,
        out_specs=out_specs, scratch_shapes=list(scratch_shapes))


NORM_ROWS = 256
N_PROMPT_BLOCKS = M_PROMPT // NORM_ROWS


def _row_spec():
    return pl.BlockSpec((NORM_ROWS, D_MODEL), lambda i, l: (i, 0))


def _prompt_row_spec():
    return pl.BlockSpec((NORM_ROWS, D_MODEL),
                        lambda i, l: (jnp.minimum(i, N_PROMPT_BLOCKS - 1), 0))


def _sample_row_spec():
    return pl.BlockSpec((NORM_ROWS, D_MODEL),
                        lambda i, l: (jnp.maximum(i - N_PROMPT_BLOCKS, 0), 0))


def _gain_spec(shift=0):
    return pl.BlockSpec((None, 1, D_MODEL), lambda i, l: (l[0] + shift, 0, 0))


def _pick_rows(xp_ref, xs_ref):
    return jnp.where(pl.program_id(0) < N_PROMPT_BLOCKS, xp_ref[...], xs_ref[...])


def _rms(x, g):
    return x * lax.rsqrt(jnp.mean(x * x, axis=-1, keepdims=True) + RMS_EPS) * g


def _norm_kernel(l_ref, xp_ref, xs_ref, g_ref, h_ref):
    h_ref[...] = _rms(_pick_rows(xp_ref, xs_ref), g_ref[...]).astype(BF16)


def _first_norm(lidx, xp, xs, g_pre):
    return pl.pallas_call(
        _norm_kernel,
        grid_spec=_grid_spec((M_ROWS // NORM_ROWS,),
                             [_prompt_row_spec(), _sample_row_spec(), _gain_spec()], _row_spec()),
        out_shape=jax.ShapeDtypeStruct((M_ROWS, D_MODEL), BF16),
        compiler_params=_params(1, NORM_ROWS * D_MODEL * 28),
        name="first_norm",
    )(lidx, xp, xs, g_pre)


def _resid_norm_body(x, o_ref, gpost_ref, gnext_ref, xn_ref, h_ref):
    xn = x + _rms(o_ref[...], gpost_ref[...])
    xn_ref[...] = xn
    h_ref[...] = _rms(xn, gnext_ref[...]).astype(BF16)


def _resid_norm_kernel(l_ref, x_ref, o_ref, gpost_ref, gnext_ref, xn_ref, h_ref):
    _resid_norm_body(x_ref[...], o_ref, gpost_ref, gnext_ref, xn_ref, h_ref)


def _resid_norm_split_kernel(l_ref, xp_ref, xs_ref, o_ref, gpost_ref, gnext_ref, xn_ref, h_ref):
    _resid_norm_body(_pick_rows(xp_ref, xs_ref), o_ref, gpost_ref, gnext_ref, xn_ref, h_ref)


def _resid_norm(lidx, x, o, g_post, g_next, next_shift):
    if isinstance(x, tuple):
        kern, x_specs = _resid_norm_split_kernel, [_prompt_row_spec(), _sample_row_spec()]
    else:
        kern, x_specs, x = _resid_norm_kernel, [_row_spec()], (x,)
    return pl.pallas_call(
        kern,
        grid_spec=_grid_spec((M_ROWS // NORM_ROWS,),
                             x_specs + [_row_spec(), _gain_spec(), _gain_spec(next_shift)],
                             [_row_spec(), _row_spec()]),
        out_shape=[jax.ShapeDtypeStruct((M_ROWS, D_MODEL), F32),
                   jax.ShapeDtypeStruct((M_ROWS, D_MODEL), BF16)],
        compiler_params=_params(1, NORM_ROWS * D_MODEL * 48),
        name="resid_norm",
    )(lidx, *x, o, g_post, g_next)


def _resid_split_kernel(l_ref, x_ref, o_ref, gpost_ref, yp_ref, ys_ref):
    xn = x_ref[...] + _rms(o_ref[...], gpost_ref[...])
    is_prompt = pl.program_id(0) < N_PROMPT_BLOCKS

    @pl.when(is_prompt)
    def _():
        yp_ref[...] = xn

    @pl.when(jnp.logical_not(is_prompt))
    def _():
        ys_ref[...] = xn


def _resid_split(lidx, x, o, g_post):
    return pl.pallas_call(
        _resid_split_kernel,
        grid_spec=_grid_spec((M_ROWS // NORM_ROWS,), [_row_spec(), _row_spec(), _gain_spec()],
                             [_prompt_row_spec(), _sample_row_spec()]),
        out_shape=[jax.ShapeDtypeStruct((M_PROMPT, D_MODEL), F32),
                   jax.ShapeDtypeStruct((M_SAMPLE, D_MODEL), F32)],
        compiler_params=_params(1, NORM_ROWS * D_MODEL * 40),
        name="resid",
    )(lidx, x, o, g_post)


MM_BM = 2176


def _resident(block_shape, index_map):
    return pl.BlockSpec(block_shape, index_map, pipeline_mode=pl.Buffered(1))


def _mm_kernel(l_ref, x_ref, w_ref, o_ref):
    o_ref[...] = jnp.dot(x_ref[...], w_ref[...].astype(BF16), preferred_element_type=F32)


def _mm_acc_kernel(l_ref, x_ref, w_ref, acc_ref, o_ref):
    o_ref[...] = acc_ref[...] + jnp.dot(x_ref[...], w_ref[...].astype(BF16),
                                        preferred_element_type=F32)


def _matmul(lidx, x, w, *, n_out, bn, k_block=0, k_blocks=1, acc=None, name):
    m = x.shape[0]
    bm = MM_BM
    kt = x.shape[1] // k_blocks
    x_spec = _resident((bm, kt), lambda i, j, l: (i, k_block))
    w_spec = pl.BlockSpec((None, kt, bn), lambda i, j, l: (l[0], k_block, j))
    o_spec = pl.BlockSpec((bm, bn), lambda i, j, l: (i, j))
    vmem = VMEM_LIMIT_CAP
    if acc is None:
        kern, in_specs, args, aliases = _mm_kernel, [x_spec, w_spec], (lidx, x, w), {}
    else:
        kern, in_specs, args = _mm_acc_kernel, [x_spec, w_spec, o_spec], (lidx, x, w, acc)
        aliases = {3: 0}
        vmem += 2 * bm * bn * 4
    return pl.pallas_call(
        kern,
        grid_spec=_grid_spec((m // bm, n_out // bn), in_specs, o_spec),
        out_shape=jax.ShapeDtypeStruct((m, n_out), F32),
        input_output_aliases=aliases,
        compiler_params=_params(2, vmem),
        name=name,
    )(*args)


FFN_BN = 256


def _ffn_in_kernel(l_ref, x_ref, wg_ref, wu_ref, o_ref):
    x = x_ref[...]
    gate = jnp.dot(x, wg_ref[...].astype(BF16), preferred_element_type=F32)
    up = jnp.dot(x, wu_ref[...].astype(BF16), preferred_element_type=F32)
    o_ref[...] = (gate * jax.nn.sigmoid(gate) * up).astype(BF16)


def _ffn_in(lidx, h, w_ffn_in):
    nb = D_FF // FFN_BN
    bm = MM_BM
    x_spec = _resident((bm, D_MODEL), lambda i, j, l: (i, 0))
    wg_spec = pl.BlockSpec((None, D_MODEL, FFN_BN), lambda i, j, l: (l[0], 0, j))
    wu_spec = pl.BlockSpec((None, D_MODEL, FFN_BN), lambda i, j, l: (l[0], 0, j + nb))
    o_spec = pl.BlockSpec((bm, FFN_BN), lambda i, j, l: (i, j))
    vmem = VMEM_LIMIT_CAP
    return pl.pallas_call(
        _ffn_in_kernel,
        grid_spec=_grid_spec((M_ROWS // bm, nb), [x_spec, wg_spec, wu_spec], o_spec),
        out_shape=jax.ShapeDtypeStruct((M_ROWS, D_FF), BF16),
        compiler_params=_params(2, vmem),
        name="ffn_in",
    )(lidx, h, w_ffn_in, w_ffn_in)


GM_BN = 256
GM_ROW_CHUNKS = 4


def _gate_merge_kernel(l_ref, h_ref, y_ref, wg_ref, p_ref, o_ref, acc_ref):
    n = pl.program_id(2)

    @pl.when(n == 0)
    def _():
        acc_ref[...] = jnp.zeros_like(acc_ref)

    wg = wg_ref[...].astype(BF16)
    p = p_ref[...].astype(BF16)
    chunk = acc_ref.shape[0] // GM_ROW_CHUNKS
    for c in range(GM_ROW_CHUNKS):
        rows = slice(c * chunk, (c + 1) * chunk)
        gate = jnp.dot(h_ref[rows, :], wg, preferred_element_type=F32)
        proj = jnp.dot(y_ref[n, rows, :], p, preferred_element_type=F32)
        acc_ref[rows, :] += jax.nn.sigmoid(gate) * proj

    @pl.when(n == pl.num_programs(2) - 1)
    def _():
        o_ref[...] = acc_ref[...].astype(BF16)


def _gate_merge(lidx, h, y, w_in, w_branch):
    bm = MM_BM
    gate0 = N_MIX // GM_BN
    per_gate = D_MODEL // GM_BN
    h_spec = _resident((bm, D_MODEL), lambda i, j, n, l: (i, 0))
    y_spec = _resident((4, bm, D_BR), lambda i, j, n, l: (0, i, 0))
    wg_spec = pl.BlockSpec((None, D_MODEL, GM_BN),
                           lambda i, j, n, l: (l[0], 0, gate0 + n * per_gate + j))
    p_spec = pl.BlockSpec((None, None, D_BR, GM_BN), lambda i, j, n, l: (l[0], n, 0, j))
    o_spec = pl.BlockSpec((bm, GM_BN), lambda i, j, n, l: (i, j))
    vmem = VMEM_LIMIT_CAP
    return pl.pallas_call(
        _gate_merge_kernel,
        grid_spec=_grid_spec((M_ROWS // bm, per_gate, 4), [h_spec, y_spec, wg_spec, p_spec],
                             o_spec, [pltpu.VMEM((bm, GM_BN), F32)]),
        out_shape=jax.ShapeDtypeStruct((M_ROWS, D_MODEL), BF16),
        compiler_params=_params(3, vmem),
        name="gate_merge",
    )(lidx, h, y, w_in, w_branch)


def _gelu(x):
    return 0.5 * x * (1.0 + lax.erf(x * (1.0 / math.sqrt(2.0))))


def _layer_norm(x, g, b):
    mu = jnp.mean(x, axis=-1, keepdims=True)
    xc = x - mu
    var = jnp.mean(xc * xc, axis=-1, keepdims=True)
    return xc * lax.rsqrt(var + LN_EPS) * g + b


def _silu(x):
    return x * jax.nn.sigmoid(x)


MIX_TB = 256
HALO_A, HALO_B, HALO_D = 8, 16, 32


def _mix_prompt_kernel(l_ref, ab_ref, ac_ref, ah_ref, p_ref, cu_ref, cv_ref, da_ref, dg_ref,
                       ach_ref, ahh_ref, ph_ref, dah_ref, dgh_ref,
                       caw_ref, pw_ref, ps_ref, sng_ref, snb_ref, sw_ref, sbt_ref,
                       cdw_ref, cdb_ref, cng_ref, cnb_ref,
                       y_ref, na_ref, nb_ref, nd_ref,
                       qpad, ppad, dpad, conv_scr):
    t = pl.program_id(1)
    has_past = t > 0
    tb = MIX_TB

    qpad[0:HALO_A, :] = jnp.where(has_past, ach_ref[...] * ahh_ref[...], 0.0)
    qpad[HALO_A:, :] = ac_ref[...] * ah_ref[...]
    conv_a = (caw_ref[0:1, :] * qpad[HALO_A - 2:HALO_A - 2 + tb, :]
              + caw_ref[1:2, :] * qpad[HALO_A - 1:HALO_A - 1 + tb, :]
              + caw_ref[2:3, :] * qpad[HALO_A:HALO_A + tb, :])
    y_ref[0] = (ab_ref[...] * conv_a).astype(BF16)

    ppad[0:HALO_B, :] = jnp.where(has_past, ph_ref[...], 0.0)
    ppad[HALO_B:, :] = p_ref[...]
    pos = t * tb + lax.broadcasted_iota(jnp.int32, (tb, 1), 0)
    for gi, w in enumerate(POOL_WINDOWS):
        cols = slice(gi * POOL_GROUP, (gi + 1) * POOL_GROUP)
        s = ppad[HALO_B:HALO_B + tb, cols]
        for i in range(1, w):
            s = s + ppad[HALO_B - i:HALO_B - i + tb, cols]
        cnt = jnp.minimum(w, pos + 1).astype(F32)
        pooled = s / cnt - ppad[HALO_B:HALO_B + tb, cols]
        yb = jnp.dot(pooled.astype(BF16), pw_ref[gi].astype(BF16), preferred_element_type=F32)
        y_ref[1, :, cols] = (yb * ps_ref[:, cols]).astype(BF16)

    u = _gelu(cu_ref[...])
    v = _layer_norm(_gelu(cv_ref[...]), sng_ref[...], snb_ref[...]).astype(BF16)
    tril = (lax.broadcasted_iota(jnp.int32, (CHUNK, CHUNK), 0)
            >= lax.broadcasted_iota(jnp.int32, (CHUNK, CHUNK), 1))
    for h in range(N_SGU_HEADS):
        w_h = jnp.where(tril, sw_ref[h], 0.0).astype(BF16)
        bias = sbt_ref[:, h:h + 1]
        hc = slice(h * SGU_HEAD, (h + 1) * SGU_HEAD)
        for c in range(tb // CHUNK):
            rows = slice(c * CHUNK, (c + 1) * CHUNK)
            s = jnp.dot(w_h, v[rows, hc], preferred_element_type=F32) + bias
            y_ref[2, rows, hc] = (u[rows, hc] * s).astype(BF16)

    dpad[0:HALO_D, :] = jnp.where(has_past, dah_ref[...] * jax.nn.sigmoid(dgh_ref[...]), 0.0)
    dpad[HALO_D:, :] = da_ref[...] * jax.nn.sigmoid(dg_ref[...])
    base = HALO_D - (CONV_D - 1)
    for c in range(D_BR // LANES):
        cols = slice(c * LANES, (c + 1) * LANES)
        acc = cdw_ref[0:1, cols] * dpad[base:base + tb, cols]
        for k in range(1, CONV_D):
            acc = acc + cdw_ref[k:k + 1, cols] * dpad[base + k:base + k + tb, cols]
        conv_scr[:, cols] = acc
    conv_d = conv_scr[...] + cdb_ref[...]
    y_ref[3] = _silu(_layer_norm(conv_d, cng_ref[...], cnb_ref[...])).astype(BF16)

    @pl.when(t == pl.num_programs(1) - 1)
    def _():
        na_ref[...] = qpad[HALO_A + tb - (CONV_A - 1):HALO_A + tb, :]
        nb_ref[...] = ppad[HALO_B + tb - POOL_BUF:HALO_B + tb, :]
        nd_ref[...] = dpad[HALO_D + tb - (CONV_D - 1):HALO_D + tb, :]


def _layer_param_specs(index_args):
    def spec(shape):
        zeros = (0,) * len(shape)
        if index_args == 2:
            return pl.BlockSpec((None,) + shape, lambda b, t, l: (l[0],) + zeros)
        return pl.BlockSpec((None,) + shape, lambda s, l: (l[0],) + zeros)
    return spec


def _mix_prompt(lidx, z, prm):
    tb = MIX_TB
    n_t = SEQ // tb

    def cur(c):
        return pl.BlockSpec((tb, D_BR), lambda b, t, l: (b * n_t + t, c))

    def halo(c, rows):
        per = tb // rows
        return pl.BlockSpec(
            (rows, D_BR), lambda b, t, l: (jnp.maximum((b * n_t + t) * per - 1, 0), c))

    spec = _layer_param_specs(2)
    in_specs = ([cur(c) for c in range(8)]
                + [halo(1, HALO_A), halo(2, HALO_A), halo(3, HALO_B),
                   halo(6, HALO_D), halo(7, HALO_D)]
                + [spec((CONV_A, D_BR)), spec((4, POOL_GROUP, POOL_GROUP)), spec((1, D_BR)),
                   spec((1, D_BR)), spec((1, D_BR)), spec((N_SGU_HEADS, CHUNK, CHUNK)),
                   spec((CHUNK, N_SGU_HEADS)), spec((CONV_D, D_BR)), spec((1, D_BR)),
                   spec((1, D_BR)), spec((1, D_BR))])
    out_specs = [pl.BlockSpec((4, tb, D_BR), lambda b, t, l: (0, b * n_t + t, 0)),
                 pl.BlockSpec((None, CONV_A - 1, D_BR), lambda b, t, l: (b, 0, 0)),
                 pl.BlockSpec((None, POOL_BUF, D_BR), lambda b, t, l: (b, 0, 0)),
                 pl.BlockSpec((None, CONV_D - 1, D_BR), lambda b, t, l: (b, 0, 0))]
    out_shape = [jax.ShapeDtypeStruct((4, M_ROWS, D_BR), BF16),
                 jax.ShapeDtypeStruct((BATCH, CONV_A - 1, D_BR), F32),
                 jax.ShapeDtypeStruct((BATCH, POOL_BUF, D_BR), F32),
                 jax.ShapeDtypeStruct((BATCH, CONV_D - 1, D_BR), F32)]
    scratch = [pltpu.VMEM((HALO_A + tb, D_BR), F32), pltpu.VMEM((HALO_B + tb, D_BR), F32),
               pltpu.VMEM((HALO_D + tb, D_BR), F32), pltpu.VMEM((tb, D_BR), F32)]
    vmem = 2 * 8 * tb * D_BR * 4 + 2 * tb * D_MODEL * 2 + 5 * tb * D_BR * 4 + 24 * tb * D_BR * 4
    return pl.pallas_call(
        _mix_prompt_kernel,
        grid_spec=_grid_spec((BATCH, n_t), in_specs, out_specs, scratch),
        out_shape=out_shape,
        compiler_params=_params(2, vmem),
        name="mix_prompt",
    )(lidx, *([z] * 13), prm["conv_a_w"], prm["pool_w"], prm["pool_scale"], prm["sgu_norm_g"],
      prm["sgu_norm_b"], prm["sgu_w"], prm["sgu_b_t"], prm["conv_d_w"], prm["conv_d_b"],
      prm["conv_norm_g"], prm["conv_norm_b"])


MIX_SS = 16


def _mix_sample_kernel(l_ref, ab_ref, ac_ref, ah_ref, p_ref, cu_ref, cv_ref, da_ref, dg_ref,
                       pa_ref, pb_ref, pd_ref,
                       caw_ref, pw_ref, ps_ref, sng_ref, snb_ref, swc_ref, sbc_ref,
                       cdw_ref, cdb_ref, cng_ref, cnb_ref,
                       y_ref, na_ref, nb_ref, nd_ref, nv_ref):
    ss = MIX_SS

    a_rows = ([pa_ref[r] for r in range(CONV_A - 1)]
              + [ac_ref[t] * ah_ref[t] for t in range(DEC_SEQ)])
    for t in range(DEC_SEQ):
        conv_a = (caw_ref[0:1, :] * a_rows[t] + caw_ref[1:2, :] * a_rows[t + 1]
                  + caw_ref[2:3, :] * a_rows[t + 2])
        y_ref[t, :, 0:D_BR] = (ab_ref[t] * conv_a).astype(BF16)
    for r in range(CONV_A - 1):
        na_ref[r] = a_rows[DEC_SEQ + r]

    def b_row(r, cols):
        if r < POOL_BUF:
            return pb_ref[r, :, cols]
        return p_ref[r - POOL_BUF, :, cols]

    for gi, w in enumerate(POOL_WINDOWS):
        cols = slice(gi * POOL_GROUP, (gi + 1) * POOL_GROUP)
        pooled = []
        for t in range(DEC_SEQ):
            s = b_row(POOL_BUF + t, cols)
            for i in range(1, w):
                s = s + b_row(POOL_BUF + t - i, cols)
            pooled.append(s / float(w) - b_row(POOL_BUF + t, cols))
        yb = jnp.dot(jnp.concatenate(pooled, axis=0).astype(BF16), pw_ref[gi].astype(BF16),
                     preferred_element_type=F32) * ps_ref[:, cols]
        for t in range(DEC_SEQ):
            y_ref[t, :, D_BR + gi * POOL_GROUP:D_BR + (gi + 1) * POOL_GROUP] = (
                yb[t * ss:(t + 1) * ss].astype(BF16))
    for r in range(POOL_BUF - DEC_SEQ):
        nb_ref[r] = pb_ref[r + DEC_SEQ]
    for t in range(DEC_SEQ):
        nb_ref[POOL_BUF - DEC_SEQ + t] = p_ref[t]

    v_rows = [_layer_norm(_gelu(cv_ref[t]), sng_ref[...], snb_ref[...]) for t in range(DEC_SEQ)]
    for t in range(DEC_SEQ):
        nv_ref[t] = v_rows[t]
        s = sbc_ref[t:t + 1, :]
        for j in range(t + 1):
            s = s + swc_ref[t * DEC_SEQ + j:t * DEC_SEQ + j + 1, :] * v_rows[j]
        y_ref[t, :, 2 * D_BR:3 * D_BR] = (_gelu(cu_ref[t]) * s).astype(BF16)

    hist = CONV_D - 1
    for r in range(hist - DEC_SEQ):
        nd_ref[r] = pd_ref[r + DEC_SEQ]
    for t in range(DEC_SEQ):
        nd_ref[hist - DEC_SEQ + t] = da_ref[t] * jax.nn.sigmoid(dg_ref[t])

    def d_row(r, cols):
        if r < hist:
            return pd_ref[r, :, cols]
        return nd_ref[r - DEC_SEQ, :, cols]

    for t in range(DEC_SEQ):
        parts = []
        for c in range(D_BR // LANES):
            cols = slice(c * LANES, (c + 1) * LANES)
            acc = cdb_ref[:, cols] + cdw_ref[0:1, cols] * d_row(t, cols)
            for k in range(1, CONV_D):
                acc = acc + cdw_ref[k:k + 1, cols] * d_row(t + k, cols)
            parts.append(acc)
        conv_d = jnp.concatenate(parts, axis=1)
        y_ref[t, :, 3 * D_BR:] = _silu(
            _layer_norm(conv_d, cng_ref[...], cnb_ref[...])).astype(BF16)


def _mix_sample(lidx, zs, states, prm):
    def cur(c):
        return pl.BlockSpec((DEC_SEQ, MIX_SS, D_BR), lambda s, l: (0, s, c))

    def state(rows):
        return pl.BlockSpec((None, rows, MIX_SS, D_BR), lambda s, l: (l[0], 0, s, 0))

    def new_state(rows):
        return pl.BlockSpec((rows, MIX_SS, D_BR), lambda s, l: (0, s, 0))

    spec = _layer_param_specs(1)
    in_specs = ([cur(c) for c in range(8)]
                + [state(CONV_A - 1), state(POOL_BUF), state(CONV_D - 1)]
                + [spec((CONV_A, D_BR)), spec((4, POOL_GROUP, POOL_GROUP)), spec((1, D_BR)),
                   spec((1, D_BR)), spec((1, D_BR)), spec((DEC_SEQ * DEC_SEQ, D_BR)),
                   spec((DEC_SEQ, D_BR)), spec((CONV_D, D_BR)), spec((1, D_BR)),
                   spec((1, D_BR)), spec((1, D_BR))])
    out_specs = [pl.BlockSpec((DEC_SEQ, MIX_SS, D_MODEL), lambda s, l: (0, s, 0)),
                 new_state(CONV_A - 1), new_state(POOL_BUF), new_state(CONV_D - 1),
                 new_state(DEC_SEQ)]
    out_shape = [jax.ShapeDtypeStruct((DEC_SEQ, DEC_BATCH, D_MODEL), BF16),
                 jax.ShapeDtypeStruct((CONV_A - 1, DEC_BATCH, D_BR), F32),
                 jax.ShapeDtypeStruct((POOL_BUF, DEC_BATCH, D_BR), F32),
                 jax.ShapeDtypeStruct((CONV_D - 1, DEC_BATCH, D_BR), F32),
                 jax.ShapeDtypeStruct((DEC_SEQ, DEC_BATCH, D_BR), F32)]
    state_rows = (CONV_A - 1) + POOL_BUF + (CONV_D - 1) + DEC_SEQ
    tile = MIX_SS * D_BR * 4
    vmem = (2 * 8 * DEC_SEQ * tile + 4 * state_rows * tile + 2 * DEC_SEQ * MIX_SS * D_MODEL * 2
            + 64 * tile)
    return pl.pallas_call(
        _mix_sample_kernel,
        grid_spec=_grid_spec((DEC_BATCH // MIX_SS,), in_specs, out_specs),
        out_shape=out_shape,
        compiler_params=_params(1, vmem),
        name="mix_sample",
    )(lidx, *([zs] * 8), states["a"], states["b"], states["d"],
      prm["conv_a_w"], prm["pool_w"], prm["pool_scale"], prm["sgu_norm_g"], prm["sgu_norm_b"],
      prm["sgu_w_coef"], prm["sgu_b_coef"], prm["conv_d_w"], prm["conv_d_b"],
      prm["conv_norm_g"], prm["conv_norm_b"])


def _row_param(p):
    return p.reshape(DEPTH, 1, p.shape[-1])


def kernel(x_prompt, x_sample, state_conv_a, state_pool_b, state_conv_d, w_in, conv_a_w, pool_w, pool_scale, sgu_norm_g, sgu_norm_b, sgu_w, sgu_b, conv_d_w, conv_d_b, conv_norm_g, conv_norm_b, w_branch, w_o, w_ffn_in, w_ffn_out, norm_mix_pre, norm_mix_post, norm_ffn_pre, norm_ffn_post):
    x = (x_prompt.reshape(M_PROMPT, D_MODEL), x_sample.reshape(M_SAMPLE, D_MODEL))

    sw = sgu_w[:, :, :DEC_SEQ, :DEC_SEQ]
    sgu_w_coef = jnp.repeat(jnp.transpose(sw, (0, 2, 3, 1)), SGU_HEAD, axis=-1)
    sgu_w_coef = sgu_w_coef.reshape(DEPTH, DEC_SEQ * DEC_SEQ, D_BR)
    sgu_b_coef = jnp.repeat(jnp.transpose(sgu_b[:, :, :DEC_SEQ], (0, 2, 1)), SGU_HEAD, axis=-1)

    prm = {
        "conv_a_w": conv_a_w, "pool_w": pool_w, "pool_scale": _row_param(pool_scale),
        "sgu_norm_g": _row_param(sgu_norm_g), "sgu_norm_b": _row_param(sgu_norm_b),
        "sgu_w": sgu_w, "sgu_b_t": jnp.transpose(sgu_b, (0, 2, 1)),
        "sgu_w_coef": sgu_w_coef, "sgu_b_coef": sgu_b_coef,
        "conv_d_w": conv_d_w, "conv_d_b": _row_param(conv_d_b),
        "conv_norm_g": _row_param(conv_norm_g), "conv_norm_b": _row_param(conv_norm_b),
    }
    states = {"a": jnp.transpose(state_conv_a, (0, 2, 1, 3)),
              "b": jnp.transpose(state_pool_b, (0, 2, 1, 3)),
              "d": jnp.transpose(state_conv_d, (0, 2, 1, 3))}
    g_mix_pre, g_mix_post = _row_param(norm_mix_pre), _row_param(norm_mix_post)
    g_ffn_pre, g_ffn_post = _row_param(norm_ffn_pre), _row_param(norm_ffn_post)

    outs = {k: [] for k in ("a_p", "a_s", "b_p", "b_s", "d_p", "d_s", "v_s")}
    h = None
    for layer in range(DEPTH):
        lidx = jnp.full((1,), layer, jnp.int32)
        if layer == 0:
            h = _first_norm(lidx, *x, g_mix_pre)
        z = _matmul(lidx, h, w_in, n_out=N_MIX, bn=512, name="proj_in")
        y, a_p, b_p, d_p = _mix_prompt(lidx, z, prm)
        zs = jnp.transpose(z[M_PROMPT:].reshape(DEC_BATCH, DEC_SEQ, N_MIX), (1, 0, 2))
        ys, a_s, b_s, d_s, v_s = _mix_sample(lidx, zs, states, prm)
        ys = jnp.transpose(ys.reshape(DEC_SEQ, DEC_BATCH, 4, D_BR), (2, 1, 0, 3))
        y = lax.dynamic_update_slice(y, ys.reshape(4, M_SAMPLE, D_BR), (0, M_PROMPT, 0))
        merged = _gate_merge(lidx, h, y, w_in, w_branch)
        mix = _matmul(lidx, merged, w_o, n_out=D_MODEL, bn=512, name="proj_o")
        x, h = _resid_norm(lidx, x, mix, g_mix_post, g_ffn_pre, 0)
        act = _ffn_in(lidx, h, w_ffn_in)
        ffn = _matmul(lidx, act, w_ffn_out, n_out=D_MODEL, bn=256, k_block=0, k_blocks=2,
                      name="ffn_out_lo")
        ffn = _matmul(lidx, act, w_ffn_out, n_out=D_MODEL, bn=256, k_block=1, k_blocks=2,
                      acc=ffn, name="ffn_out_hi")
        if layer + 1 < DEPTH:
            x, h = _resid_norm(lidx, x, ffn, g_ffn_post, g_mix_pre, 1)
        else:
            y_prompt, y_sample = _resid_split(lidx, x, ffn, g_ffn_post)
        a_s, b_s, d_s, v_s = (jnp.transpose(s, (1, 0, 2)) for s in (a_s, b_s, d_s, v_s))
        for k, val in zip(outs, (a_p, a_s, b_p, b_s, d_p, d_s, v_s)):
            outs[k].append(val)

    y_prompt = y_prompt.reshape(BATCH, SEQ, D_MODEL)
    y_sample = y_sample.reshape(DEC_BATCH, DEC_SEQ, D_MODEL)
    return (y_prompt, y_sample) + tuple(jnp.stack(outs[k]) for k in outs)
```

```python
import math

import jax
import jax.numpy as jnp
from jax import lax
from jax.experimental import pallas as pl
from jax.experimental.pallas import tpu as pltpu

D_MODEL = 4096
BATCH = 4
SEQ = 2048
DEPTH = 4
DEC_BATCH = 128
DEC_SEQ = 4
PAST_LEN = 16384
D_BR = D_MODEL // 4
CONV_A = 3
POOL_WINDOWS = (2, 4, 8, 16)
POOL_GROUP = D_BR // len(POOL_WINDOWS)
POOL_BUF = max(POOL_WINDOWS) - 1
CHUNK = 128
N_SGU_HEADS = 8
SGU_HEAD = D_BR // N_SGU_HEADS
CONV_D = 31
N_MIX = 8 * D_BR
N_IN = N_MIX + 4 * D_MODEL
D_FF = 11008
RMS_EPS = 1e-6
LN_EPS = 1e-5

M_PROMPT = BATCH * SEQ
M_SAMPLE = DEC_BATCH * DEC_SEQ
M_ROWS = M_PROMPT + M_SAMPLE

LANES = 128
SUBLANES = 8
VMEM_LIMIT_CAP = 58 * 1024 * 1024

F32 = jnp.float32
BF16 = jnp.bfloat16


def _params(n_grid, vmem_bytes):
    limit = min(int(vmem_bytes * 1.15) + (4 << 20), VMEM_LIMIT_CAP)
    return pltpu.CompilerParams(
        dimension_semantics=("arbitrary",) * n_grid, vmem_limit_bytes=limit)


def _grid_spec(grid, in_specs, out_specs, scratch_shapes=()):
    return pltpu.PrefetchScalarGridSpec(
        num_scalar_prefetch=1, grid=grid, in_specs=in_specs,
        out_specs=out_specs, scratch_shapes=list(scratch_shapes))


NORM_ROWS = 256
N_PROMPT_BLOCKS = M_PROMPT // NORM_ROWS


def _row_spec():
    return pl.BlockSpec((NORM_ROWS, D_MODEL), lambda i, l: (i, 0))


def _prompt_row_spec():
    return pl.BlockSpec((NORM_ROWS, D_MODEL),
                        lambda i, l: (jnp.minimum(i, N_PROMPT_BLOCKS - 1), 0))


def _sample_row_spec():
    return pl.BlockSpec((NORM_ROWS, D_MODEL),
                        lambda i, l: (jnp.maximum(i - N_PROMPT_BLOCKS, 0), 0))


def _gain_spec(shift=0):
    return pl.BlockSpec((None, 1, D_MODEL), lambda i, l: (l[0] + shift, 0, 0))


def _pick_rows(xp_ref, xs_ref):
    return jnp.where(pl.program_id(0) < N_PROMPT_BLOCKS, xp_ref[...], xs_ref[...])


def _rms(x, g):
    return x * lax.rsqrt(jnp.mean(x * x, axis=-1, keepdims=True) + RMS_EPS) * g


def _norm_kernel(l_ref, xp_ref, xs_ref, g_ref, h_ref):
    h_ref[...] = _rms(_pick_rows(xp_ref, xs_ref), g_ref[...]).astype(BF16)


def _first_norm(lidx, xp, xs, g_pre):
    return pl.pallas_call(
        _norm_kernel,
        grid_spec=_grid_spec((M_ROWS // NORM_ROWS,),
                             [_prompt_row_spec(), _sample_row_spec(), _gain_spec()], _row_spec()),
        out_shape=jax.ShapeDtypeStruct((M_ROWS, D_MODEL), BF16),
        compiler_params=_params(1, NORM_ROWS * D_MODEL * 28),
        name="first_norm",
    )(lidx, xp, xs, g_pre)


def _resid_norm_body(x, o_ref, gpost_ref, gnext_ref, xn_ref, h_ref):
    xn = x + _rms(o_ref[...], gpost_ref[...])
    xn_ref[...] = xn
    h_ref[...] = _rms(xn, gnext_ref[...]).astype(BF16)


def _resid_norm_kernel(l_ref, x_ref, o_ref, gpost_ref, gnext_ref, xn_ref, h_ref):
    _resid_norm_body(x_ref[...], o_ref, gpost_ref, gnext_ref, xn_ref, h_ref)


def _resid_norm_split_kernel(l_ref, xp_ref, xs_ref, o_ref, gpost_ref, gnext_ref, xn_ref, h_ref):
    _resid_norm_body(_pick_rows(xp_ref, xs_ref), o_ref, gpost_ref, gnext_ref, xn_ref, h_ref)


def _resid_norm(lidx, x, o, g_post, g_next, next_shift):
    if isinstance(x, tuple):
        kern, x_specs = _resid_norm_split_kernel, [_prompt_row_spec(), _sample_row_spec()]
    else:
        kern, x_specs, x = _resid_norm_kernel, [_row_spec()], (x,)
    return pl.pallas_call(
        kern,
        grid_spec=_grid_spec((M_ROWS // NORM_ROWS,),
                             x_specs + [_row_spec(), _gain_spec(), _gain_spec(next_shift)],
                             [_row_spec(), _row_spec()]),
        out_shape=[jax.ShapeDtypeStruct((M_ROWS, D_MODEL), F32),
                   jax.ShapeDtypeStruct((M_ROWS, D_MODEL), BF16)],
        compiler_params=_params(1, NORM_ROWS * D_MODEL * 48),
        name="resid_norm",
    )(lidx, *x, o, g_post, g_next)


def _resid_split_kernel(l_ref, x_ref, o_ref, gpost_ref, yp_ref, ys_ref):
    xn = x_ref[...] + _rms(o_ref[...], gpost_ref[...])
    is_prompt = pl.program_id(0) < N_PROMPT_BLOCKS

    @pl.when(is_prompt)
    def _():
        yp_ref[...] = xn

    @pl.when(jnp.logical_not(is_prompt))
    def _():
        ys_ref[...] = xn


def _resid_split(lidx, x, o, g_post):
    return pl.pallas_call(
        _resid_split_kernel,
        grid_spec=_grid_spec((M_ROWS // NORM_ROWS,), [_row_spec(), _row_spec(), _gain_spec()],
                             [_prompt_row_spec(), _sample_row_spec()]),
        out_shape=[jax.ShapeDtypeStruct((M_PROMPT, D_MODEL), F32),
                   jax.ShapeDtypeStruct((M_SAMPLE, D_MODEL), F32)],
        compiler_params=_params(1, NORM_ROWS * D_MODEL * 40),
        name="resid",
    )(lidx, x, o, g_post)


MM_BM = 2176


def _resident(block_shape, index_map):
    return pl.BlockSpec(block_shape, index_map, pipeline_mode=pl.Buffered(1))


def _mm_kernel(l_ref, x_ref, w_ref, o_ref):
    o_ref[...] = jnp.dot(x_ref[...], w_ref[...].astype(BF16), preferred_element_type=F32)


def _mm_acc_kernel(l_ref, x_ref, w_ref, acc_ref, o_ref):
    o_ref[...] = acc_ref[...] + jnp.dot(x_ref[...], w_ref[...].astype(BF16),
                                        preferred_element_type=F32)


def _matmul(lidx, x, w, *, n_out, bn, k_block=0, k_blocks=1, acc=None, name):
    m = x.shape[0]
    bm = MM_BM
    kt = x.shape[1] // k_blocks
    x_spec = _resident((bm, kt), lambda i, j, l: (i, k_block))
    w_spec = pl.BlockSpec((None, kt, bn), lambda i, j, l: (l[0], k_block, j))
    o_spec = pl.BlockSpec((bm, bn), lambda i, j, l: (i, j))
    if acc is None:
        kern, in_specs, args, aliases = _mm_kernel, [x_spec, w_spec], (lidx, x, w), {}
    else:
        kern, in_specs, args = _mm_acc_kernel, [x_spec, w_spec, o_spec], (lidx, x, w, acc)
        aliases = {3: 0}
    return pl.pallas_call(
        kern,
        grid_spec=_grid_spec((m // bm, n_out // bn), in_specs, o_spec),
        out_shape=jax.ShapeDtypeStruct((m, n_out), F32),
        input_output_aliases=aliases,
        compiler_params=_params(2, VMEM_LIMIT_CAP),
        name=name,
    )(*args)


FFN_BN = 256


def _ffn_in_kernel(l_ref, x_ref, wg_ref, wu_ref, o_ref):
    x = x_ref[...]
    gate = jnp.dot(x, wg_ref[...].astype(BF16), preferred_element_type=F32)
    up = jnp.dot(x, wu_ref[...].astype(BF16), preferred_element_type=F32)
    o_ref[...] = (gate * jax.nn.sigmoid(gate) * up).astype(BF16)


def _ffn_in(lidx, h, w_ffn_in):
    nb = D_FF // FFN_BN
    bm = MM_BM
    x_spec = _resident((bm, D_MODEL), lambda i, j, l: (i, 0))
    wg_spec = pl.BlockSpec((None, D_MODEL, FFN_BN), lambda i, j, l: (l[0], 0, j))
    wu_spec = pl.BlockSpec((None, D_MODEL, FFN_BN), lambda i, j, l: (l[0], 0, j + nb))
    o_spec = pl.BlockSpec((bm, FFN_BN), lambda i, j, l: (i, j))
    return pl.pallas_call(
        _ffn_in_kernel,
        grid_spec=_grid_spec((M_ROWS // bm, nb), [x_spec, wg_spec, wu_spec], o_spec),
        out_shape=jax.ShapeDtypeStruct((M_ROWS, D_FF), BF16),
        compiler_params=_params(2, VMEM_LIMIT_CAP),
        name="ffn_in",
    )(lidx, h, w_ffn_in, w_ffn_in)


GM_BN = 256
GM_ROW_CHUNKS = 4


def _gate_merge_kernel(l_ref, h_ref, y_ref, wg_ref, p_ref, o_ref, acc_ref):
    n = pl.program_id(2)

    @pl.when(n == 0)
    def _():
        acc_ref[...] = jnp.zeros_like(acc_ref)

    wg = wg_ref[...].astype(BF16)
    p = p_ref[...].astype(BF16)
    chunk = acc_ref.shape[0] // GM_ROW_CHUNKS
    for c in range(GM_ROW_CHUNKS):
        rows = slice(c * chunk, (c + 1) * chunk)
        gate = jnp.dot(h_ref[rows, :], wg, preferred_element_type=F32)
        proj = jnp.dot(y_ref[n, rows, :], p, preferred_element_type=F32)
        acc_ref[rows, :] += jax.nn.sigmoid(gate) * proj

    @pl.when(n == pl.num_programs(2) - 1)
    def _():
        o_ref[...] = acc_ref[...].astype(BF16)


def _gate_merge(lidx, h, y, w_in, w_branch):
    bm = MM_BM
    gate0 = N_MIX // GM_BN
    per_gate = D_MODEL // GM_BN
    h_spec = _resident((bm, D_MODEL), lambda i, j, n, l: (i, 0))
    y_spec = _resident((4, bm, D_BR), lambda i, j, n, l: (0, i, 0))
    wg_spec = pl.BlockSpec((None, D_MODEL, GM_BN),
                           lambda i, j, n, l: (l[0], 0, gate0 + n * per_gate + j))
    p_spec = pl.BlockSpec((None, None, D_BR, GM_BN), lambda i, j, n, l: (l[0], n, 0, j))
    o_spec = pl.BlockSpec((bm, GM_BN), lambda i, j, n, l: (i, j))
    vmem = VMEM_LIMIT_CAP
    return pl.pallas_call(
        _gate_merge_kernel,
        grid_spec=_grid_spec((M_ROWS // bm, per_gate, 4), [h_spec, y_spec, wg_spec, p_spec],
                             o_spec, [pltpu.VMEM((bm, GM_BN), F32)]),
        out_shape=jax.ShapeDtypeStruct((M_ROWS, D_MODEL), BF16),
        compiler_params=_params(3, vmem),
        name="gate_merge",
    )(lidx, h, y, w_in, w_branch)


def _gelu(x):
    return 0.5 * x * (1.0 + lax.erf(x * (1.0 / math.sqrt(2.0))))


def _layer_norm(x, g, b):
    mu = jnp.mean(x, axis=-1, keepdims=True)
    xc = x - mu
    var = jnp.mean(xc * xc, axis=-1, keepdims=True)
    return xc * lax.rsqrt(var + LN_EPS) * g + b


def _silu(x):
    return x * jax.nn.sigmoid(x)


MIX_TB = 256
HALO_A, HALO_B, HALO_D = 8, 16, 32


def _mix_prompt_kernel(l_ref, ab_ref, ac_ref, ah_ref, p_ref, cu_ref, cv_ref, da_ref, dg_ref,
                       ach_ref, ahh_ref, ph_ref, dah_ref, dgh_ref,
                       caw_ref, pw_ref, ps_ref, sng_ref, snb_ref, sw_ref, sbt_ref,
                       cdw_ref, cdb_ref, cng_ref, cnb_ref,
                       y_ref, na_ref, nb_ref, nd_ref,
                       qpad, ppad, dpad, conv_scr, phase_scr):
    t = pl.program_id(1)
    has_past = t > 0
    tb = MIX_TB

    qpad[0:HALO_A, :] = jnp.where(has_past, ach_ref[...] * ahh_ref[...], 0.0)
    qpad[HALO_A:, :] = ac_ref[...] * ah_ref[...]
    conv_a = (caw_ref[0:1, :] * qpad[HALO_A - 2:HALO_A - 2 + tb, :]
              + caw_ref[1:2, :] * qpad[HALO_A - 1:HALO_A - 1 + tb, :]
              + caw_ref[2:3, :] * qpad[HALO_A:HALO_A + tb, :])
    y_ref[0] = (ab_ref[...] * conv_a).astype(BF16)

    ppad[0:HALO_B, :] = jnp.where(has_past, ph_ref[...], 0.0)
    ppad[HALO_B:, :] = p_ref[...]
    pos = t * tb + lax.broadcasted_iota(jnp.int32, (tb, 1), 0)
    for gi, w in enumerate(POOL_WINDOWS):
        cols = slice(gi * POOL_GROUP, (gi + 1) * POOL_GROUP)
        s = ppad[HALO_B:HALO_B + tb, cols]
        for i in range(1, w):
            s = s + ppad[HALO_B - i:HALO_B - i + tb, cols]
        cnt = jnp.minimum(w, pos + 1).astype(F32)
        pooled = s / cnt - ppad[HALO_B:HALO_B + tb, cols]
        yb = jnp.dot(pooled.astype(BF16), pw_ref[gi].astype(BF16), preferred_element_type=F32)
        y_ref[1, :, cols] = (yb * ps_ref[:, cols]).astype(BF16)

    u = _gelu(cu_ref[...])
    v = _layer_norm(_gelu(cv_ref[...]), sng_ref[...], snb_ref[...]).astype(BF16)
    tril = (lax.broadcasted_iota(jnp.int32, (CHUNK, CHUNK), 0)
            >= lax.broadcasted_iota(jnp.int32, (CHUNK, CHUNK), 1))
    for h in range(N_SGU_HEADS):
        w_h = jnp.where(tril, sw_ref[h], 0.0).astype(BF16)
        bias = sbt_ref[:, h:h + 1]
        hc = slice(h * SGU_HEAD, (h + 1) * SGU_HEAD)
        for c in range(tb // CHUNK):
            rows = slice(c * CHUNK, (c + 1) * CHUNK)
            s = jnp.dot(w_h, v[rows, hc], preferred_element_type=F32) + bias
            y_ref[2, rows, hc] = (u[rows, hc] * s).astype(BF16)

    dpad[0:HALO_D, :] = jnp.where(has_past, dah_ref[...] * jax.nn.sigmoid(dgh_ref[...]), 0.0)
    dpad[HALO_D:, :] = da_ref[...] * jax.nn.sigmoid(dg_ref[...])
    base = HALO_D - (CONV_D - 1)
    for c in range(D_BR // LANES):
        cols = slice(c * LANES, (c + 1) * LANES)
        out = None
        for r in range(SUBLANES):
            rows = tb if r == 0 else tb + SUBLANES
            part = None
            for k in range(CONV_D):
                if (base + k) % SUBLANES != r:
                    continue
                row0 = base + k - r
                term = cdw_ref[k:k + 1, cols] * dpad[row0:row0 + rows, cols]
                part = term if part is None else part + term
            if r == 0:
                out = part
            else:
                phase_scr[r - 1] = part
        for r in range(1, SUBLANES):
            out = out + phase_scr[r - 1, r:r + tb, :]
        conv_scr[:, cols] = out
    conv_d = conv_scr[...] + cdb_ref[...]
    y_ref[3] = _silu(_layer_norm(conv_d, cng_ref[...], cnb_ref[...])).astype(BF16)

    @pl.when(t == pl.num_programs(1) - 1)
    def _():
        na_ref[...] = qpad[HALO_A + tb - (CONV_A - 1):HALO_A + tb, :]
        nb_ref[...] = ppad[HALO_B + tb - POOL_BUF:HALO_B + tb, :]
        nd_ref[...] = dpad[HALO_D + tb - (CONV_D - 1):HALO_D + tb, :]


def _layer_param_specs(index_args):
    def spec(shape):
        zeros = (0,) * len(shape)
        if index_args == 2:
            return pl.BlockSpec((None,) + shape, lambda b, t, l: (l[0],) + zeros)
        return pl.BlockSpec((None,) + shape, lambda s, l: (l[0],) + zeros)
    return spec


def _mix_prompt(lidx, z, prm):
    tb = MIX_TB
    n_t = SEQ // tb

    def cur(c):
        return pl.BlockSpec((tb, D_BR), lambda b, t, l: (b * n_t + t, c))

    def halo(c, rows):
        per = tb // rows
        return pl.BlockSpec(
            (rows, D_BR), lambda b, t, l: (jnp.maximum((b * n_t + t) * per - 1, 0), c))

    spec = _layer_param_specs(2)
    in_specs = ([cur(c) for c in range(8)]
                + [halo(1, HALO_A), halo(2, HALO_A), halo(3, HALO_B),
                   halo(6, HALO_D), halo(7, HALO_D)]
                + [spec((CONV_A, D_BR)), spec((4, POOL_GROUP, POOL_GROUP)), spec((1, D_BR)),
                   spec((1, D_BR)), spec((1, D_BR)), spec((N_SGU_HEADS, CHUNK, CHUNK)),
                   spec((CHUNK, N_SGU_HEADS)), spec((CONV_D, D_BR)), spec((1, D_BR)),
                   spec((1, D_BR)), spec((1, D_BR))])
    out_specs = [pl.BlockSpec((4, tb, D_BR), lambda b, t, l: (0, b * n_t + t, 0)),
                 pl.BlockSpec((None, CONV_A - 1, D_BR), lambda b, t, l: (b, 0, 0)),
                 pl.BlockSpec((None, POOL_BUF, D_BR), lambda b, t, l: (b, 0, 0)),
                 pl.BlockSpec((None, CONV_D - 1, D_BR), lambda b, t, l: (b, 0, 0))]
    out_shape = [jax.ShapeDtypeStruct((4, M_ROWS, D_BR), BF16),
                 jax.ShapeDtypeStruct((BATCH, CONV_A - 1, D_BR), F32),
                 jax.ShapeDtypeStruct((BATCH, POOL_BUF, D_BR), F32),
                 jax.ShapeDtypeStruct((BATCH, CONV_D - 1, D_BR), F32)]
    scratch = [pltpu.VMEM((HALO_A + tb, D_BR), F32), pltpu.VMEM((HALO_B + tb, D_BR), F32),
               pltpu.VMEM((HALO_D + tb, D_BR), F32), pltpu.VMEM((tb, D_BR), F32),
               pltpu.VMEM((SUBLANES - 1, tb + SUBLANES, LANES), F32)]
    vmem = 2 * 8 * tb * D_BR * 4 + 2 * tb * D_MODEL * 2 + 5 * tb * D_BR * 4 + 24 * tb * D_BR * 4
    return pl.pallas_call(
        _mix_prompt_kernel,
        grid_spec=_grid_spec((BATCH, n_t), in_specs, out_specs, scratch),
        out_shape=out_shape,
        compiler_params=_params(2, vmem),
        name="mix_prompt",
    )(lidx, *([z] * 13), prm["conv_a_w"], prm["pool_w"], prm["pool_scale"], prm["sgu_norm_g"],
      prm["sgu_norm_b"], prm["sgu_w"], prm["sgu_b_t"], prm["conv_d_w"], prm["conv_d_b"],
      prm["conv_norm_g"], prm["conv_norm_b"])


MIX_SS = 16


def _mix_sample_kernel(l_ref, ab_ref, ac_ref, ah_ref, p_ref, cu_ref, cv_ref, da_ref, dg_ref,
                       pa_ref, pb_ref, pd_ref,
                       caw_ref, pw_ref, ps_ref, sng_ref, snb_ref, swc_ref, sbc_ref,
                       cdw_ref, cdb_ref, cng_ref, cnb_ref,
                       y_ref, na_ref, nb_ref, nd_ref, nv_ref):
    ss = MIX_SS

    a_rows = ([pa_ref[r] for r in range(CONV_A - 1)]
              + [ac_ref[t] * ah_ref[t] for t in range(DEC_SEQ)])
    for t in range(DEC_SEQ):
        conv_a = (caw_ref[0:1, :] * a_rows[t] + caw_ref[1:2, :] * a_rows[t + 1]
                  + caw_ref[2:3, :] * a_rows[t + 2])
        y_ref[t, :, 0:D_BR] = (ab_ref[t] * conv_a).astype(BF16)
    for r in range(CONV_A - 1):
        na_ref[r] = a_rows[DEC_SEQ + r]

    def b_row(r, cols):
        if r < POOL_BUF:
            return pb_ref[r, :, cols]
        return p_ref[r - POOL_BUF, :, cols]

    for gi, w in enumerate(POOL_WINDOWS):
        cols = slice(gi * POOL_GROUP, (gi + 1) * POOL_GROUP)
        pooled = []
        for t in range(DEC_SEQ):
            s = b_row(POOL_BUF + t, cols)
            for i in range(1, w):
                s = s + b_row(POOL_BUF + t - i, cols)
            pooled.append(s / float(w) - b_row(POOL_BUF + t, cols))
        yb = jnp.dot(jnp.concatenate(pooled, axis=0).astype(BF16), pw_ref[gi].astype(BF16),
                     preferred_element_type=F32) * ps_ref[:, cols]
        for t in range(DEC_SEQ):
            y_ref[t, :, D_BR + gi * POOL_GROUP:D_BR + (gi + 1) * POOL_GROUP] = (
                yb[t * ss:(t + 1) * ss].astype(BF16))
    for r in range(POOL_BUF - DEC_SEQ):
        nb_ref[r] = pb_ref[r + DEC_SEQ]
    for t in range(DEC_SEQ):
        nb_ref[POOL_BUF - DEC_SEQ + t] = p_ref[t]

    v_rows = [_layer_norm(_gelu(cv_ref[t]), sng_ref[...], snb_ref[...]) for t in range(DEC_SEQ)]
    for t in range(DEC_SEQ):
        nv_ref[t] = v_rows[t]
        s = sbc_ref[t:t + 1, :]
        for j in range(t + 1):
            s = s + swc_ref[t * DEC_SEQ + j:t * DEC_SEQ + j + 1, :] * v_rows[j]
        y_ref[t, :, 2 * D_BR:3 * D_BR] = (_gelu(cu_ref[t]) * s).astype(BF16)

    hist = CONV_D - 1
    for r in range(hist - DEC_SEQ):
        nd_ref[r] = pd_ref[r + DEC_SEQ]
    for t in range(DEC_SEQ):
        nd_ref[hist - DEC_SEQ + t] = da_ref[t] * jax.nn.sigmoid(dg_ref[t])

    def d_row(r, cols):
        if r < hist:
            return pd_ref[r, :, cols]
        return nd_ref[r - DEC_SEQ, :, cols]

    for t in range(DEC_SEQ):
        parts = []
        for c in range(D_BR // LANES):
            cols = slice(c * LANES, (c + 1) * LANES)
            acc = cdb_ref[:, cols] + cdw_ref[0:1, cols] * d_row(t, cols)
            for k in range(1, CONV_D):
                acc = acc + cdw_ref[k:k + 1, cols] * d_row(t + k, cols)
            parts.append(acc)
        conv_d = jnp.concatenate(parts, axis=1)
        y_ref[t, :, 3 * D_BR:] = _silu(
            _layer_norm(conv_d, cng_ref[...], cnb_ref[...])).astype(BF16)


def _mix_sample(lidx, zs, states, prm):
    def cur(c):
        return pl.BlockSpec((DEC_SEQ, MIX_SS, D_BR), lambda s, l: (0, s, c))

    def state(rows):
        return pl.BlockSpec((None, rows, MIX_SS, D_BR), lambda s, l: (l[0], 0, s, 0))

    def new_state(rows):
        return pl.BlockSpec((rows, MIX_SS, D_BR), lambda s, l: (0, s, 0))

    spec = _layer_param_specs(1)
    in_specs = ([cur(c) for c in range(8)]
                + [state(CONV_A - 1), state(POOL_BUF), state(CONV_D - 1)]
                + [spec((CONV_A, D_BR)), spec((4, POOL_GROUP, POOL_GROUP)), spec((1, D_BR)),
                   spec((1, D_BR)), spec((1, D_BR)), spec((DEC_SEQ * DEC_SEQ, D_BR)),
                   spec((DEC_SEQ, D_BR)), spec((CONV_D, D_BR)), spec((1, D_BR)),
                   spec((1, D_BR)), spec((1, D_BR))])
    out_specs = [pl.BlockSpec((DEC_SEQ, MIX_SS, D_MODEL), lambda s, l: (0, s, 0)),
                 new_state(CONV_A - 1), new_state(POOL_BUF), new_state(CONV_D - 1),
                 new_state(DEC_SEQ)]
    out_shape = [jax.ShapeDtypeStruct((DEC_SEQ, DEC_BATCH, D_MODEL), BF16),
                 jax.ShapeDtypeStruct((CONV_A - 1, DEC_BATCH, D_BR), F32),
                 jax.ShapeDtypeStruct((POOL_BUF, DEC_BATCH, D_BR), F32),
                 jax.ShapeDtypeStruct((CONV_D - 1, DEC_BATCH, D_BR), F32),
                 jax.ShapeDtypeStruct((DEC_SEQ, DEC_BATCH, D_BR), F32)]
    state_rows = (CONV_A - 1) + POOL_BUF + (CONV_D - 1) + DEC_SEQ
    tile = MIX_SS * D_BR * 4
    vmem = (2 * 8 * DEC_SEQ * tile + 4 * state_rows * tile + 2 * DEC_SEQ * MIX_SS * D_MODEL * 2
            + 64 * tile)
    return pl.pallas_call(
        _mix_sample_kernel,
        grid_spec=_grid_spec((DEC_BATCH // MIX_SS,), in_specs, out_specs),
        out_shape=out_shape,
        compiler_params=_params(1, vmem),
        name="mix_sample",
    )(lidx, *([zs] * 8), states["a"], states["b"], states["d"],
      prm["conv_a_w"], prm["pool_w"], prm["pool_scale"], prm["sgu_norm_g"], prm["sgu_norm_b"],
      prm["sgu_w_coef"], prm["sgu_b_coef"], prm["conv_d_w"], prm["conv_d_b"],
      prm["conv_norm_g"], prm["conv_norm_b"])


def _row_param(p):
    return p.reshape(DEPTH, 1, p.shape[-1])


def kernel(x_prompt, x_sample, state_conv_a, state_pool_b, state_conv_d, w_in, conv_a_w, pool_w, pool_scale, sgu_norm_g, sgu_norm_b, sgu_w, sgu_b, conv_d_w, conv_d_b, conv_norm_g, conv_norm_b, w_branch, w_o, w_ffn_in, w_ffn_out, norm_mix_pre, norm_mix_post, norm_ffn_pre, norm_ffn_post):
    x = (x_prompt.reshape(M_PROMPT, D_MODEL), x_sample.reshape(M_SAMPLE, D_MODEL))

    sw = sgu_w[:, :, :DEC_SEQ, :DEC_SEQ]
    sgu_w_coef = jnp.repeat(jnp.transpose(sw, (0, 2, 3, 1)), SGU_HEAD, axis=-1)
    sgu_w_coef = sgu_w_coef.reshape(DEPTH, DEC_SEQ * DEC_SEQ, D_BR)
    sgu_b_coef = jnp.repeat(jnp.transpose(sgu_b[:, :, :DEC_SEQ], (0, 2, 1)), SGU_HEAD, axis=-1)

    prm = {
        "conv_a_w": conv_a_w, "pool_w": pool_w, "pool_scale": _row_param(pool_scale),
        "sgu_norm_g": _row_param(sgu_norm_g), "sgu_norm_b": _row_param(sgu_norm_b),
        "sgu_w": sgu_w, "sgu_b_t": jnp.transpose(sgu_b, (0, 2, 1)),
        "sgu_w_coef": sgu_w_coef, "sgu_b_coef": sgu_b_coef,
        "conv_d_w": conv_d_w, "conv_d_b": _row_param(conv_d_b),
        "conv_norm_g": _row_param(conv_norm_g), "conv_norm_b": _row_param(conv_norm_b),
    }
    states = {"a": jnp.transpose(state_conv_a, (0, 2, 1, 3)),
              "b": jnp.transpose(state_pool_b, (0, 2, 1, 3)),
              "d": jnp.transpose(state_conv_d, (0, 2, 1, 3))}
    g_mix_pre, g_mix_post = _row_param(norm_mix_pre), _row_param(norm_mix_post)
    g_ffn_pre, g_ffn_post = _row_param(norm_ffn_pre), _row_param(norm_ffn_post)

    outs = {k: [] for k in ("a_p", "a_s", "b_p", "b_s", "d_p", "d_s", "v_s")}
    h = None
    for layer in range(DEPTH):
        lidx = jnp.full((1,), layer, jnp.int32)
        if layer == 0:
            h = _first_norm(lidx, *x, g_mix_pre)
        z = _matmul(lidx, h, w_in, n_out=N_MIX, bn=512, name="proj_in")
        y, a_p, b_p, d_p = _mix_prompt(lidx, z, prm)
        zs = jnp.transpose(z[M_PROMPT:].reshape(DEC_BATCH, DEC_SEQ, N_MIX), (1, 0, 2))
        ys, a_s, b_s, d_s, v_s = _mix_sample(lidx, zs, states, prm)
        ys = jnp.transpose(ys.reshape(DEC_SEQ, DEC_BATCH, 4, D_BR), (2, 1, 0, 3))
        y = lax.dynamic_update_slice(y, ys.reshape(4, M_SAMPLE, D_BR), (0, M_PROMPT, 0))
        merged = _gate_merge(lidx, h, y, w_in, w_branch)
        mix = _matmul(lidx, merged, w_o, n_out=D_MODEL, bn=512, name="proj_o")
        x, h = _resid_norm(lidx, x, mix, g_mix_post, g_ffn_pre, 0)
        act = _ffn_in(lidx, h, w_ffn_in)
        ffn = _matmul(lidx, act, w_ffn_out, n_out=D_MODEL, bn=256, k_block=0, k_blocks=2,
                      name="ffn_out_lo")
        ffn = _matmul(lidx, act, w_ffn_out, n_out=D_MODEL, bn=256, k_block=1, k_blocks=2,
                      acc=ffn, name="ffn_out_hi")
        if layer + 1 < DEPTH:
            x, h = _resid_norm(lidx, x, ffn, g_ffn_post, g_mix_pre, 1)
        else:
            y_prompt, y_sample = _resid_split(lidx, x, ffn, g_ffn_post)
        for k, val in zip(outs, (a_p, a_s, b_p, b_s, d_p, d_s, v_s)):
            outs[k].append(val)

    y_prompt = y_prompt.reshape(BATCH, SEQ, D_MODEL)
    y_sample = y_sample.reshape(DEC_BATCH, DEC_SEQ, D_MODEL)
    stacked = {k: jnp.stack(v) for k, v in outs.items()}
    for k in ("a_s", "b_s", "d_s", "v_s"):
        stacked[k] = jnp.transpose(stacked[k], (0, 2, 1, 3))
    return (y_prompt, y_sample) + tuple(stacked[k] for k in outs)
```

```python
import math

import jax
import jax.numpy as jnp
from jax import lax
from jax.experimental import pallas as pl
from jax.experimental.pallas import tpu as pltpu

D_MODEL = 4096
BATCH = 4
SEQ = 2048
DEPTH = 4
DEC_BATCH = 128
DEC_SEQ = 4
PAST_LEN = 16384
D_BR = D_MODEL // 4
CONV_A = 3
POOL_WINDOWS = (2, 4, 8, 16)
POOL_GROUP = D_BR // len(POOL_WINDOWS)
POOL_BUF = max(POOL_WINDOWS) - 1
CHUNK = 128
N_SGU_HEADS = 8
SGU_HEAD = D_BR // N_SGU_HEADS
CONV_D = 31
N_MIX = 8 * D_BR
N_IN = N_MIX + 4 * D_MODEL
D_FF = 11008
RMS_EPS = 1e-6
LN_EPS = 1e-5

M_PROMPT = BATCH * SEQ
M_SAMPLE = DEC_BATCH * DEC_SEQ
M_ROWS = M_PROMPT + M_SAMPLE

LANES = 128
SUBLANES = 8
VMEM_LIMIT_CAP = 58 * 1024 * 1024

F32 = jnp.float32
BF16 = jnp.bfloat16


def _params(n_grid, vmem_bytes):
    limit = min(int(vmem_bytes * 1.15) + (4 << 20), VMEM_LIMIT_CAP)
    return pltpu.CompilerParams(
        dimension_semantics=("arbitrary",) * n_grid, vmem_limit_bytes=limit)


def _grid_spec(grid, in_specs, out_specs, scratch_shapes=()):
    return pltpu.PrefetchScalarGridSpec(
        num_scalar_prefetch=1, grid=grid, in_specs=in_specs,
        out_specs=out_specs, scratch_shapes=list(scratch_shapes))


NORM_ROWS = 256
N_PROMPT_BLOCKS = M_PROMPT // NORM_ROWS


def _row_spec():
    return pl.BlockSpec((NORM_ROWS, D_MODEL), lambda i, l: (i, 0))


def _prompt_row_spec():
    return pl.BlockSpec((NORM_ROWS, D_MODEL),
                        lambda i, l: (jnp.minimum(i, N_PROMPT_BLOCKS - 1), 0))


def _sample_row_spec():
    return pl.BlockSpec((NORM_ROWS, D_MODEL),
                        lambda i, l: (jnp.maximum(i - N_PROMPT_BLOCKS, 0), 0))


def _gain_spec(shift=0):
    return pl.BlockSpec((None, 1, D_MODEL), lambda i, l: (l[0] + shift, 0, 0))


def _pick_rows(xp_ref, xs_ref):
    return jnp.where(pl.program_id(0) < N_PROMPT_BLOCKS, xp_ref[...], xs_ref[...])


def _rms(x, g):
    return x * lax.rsqrt(jnp.mean(x * x, axis=-1, keepdims=True) + RMS_EPS) * g


def _norm_kernel(l_ref, xp_ref, xs_ref, g_ref, h_ref):
    h_ref[...] = _rms(_pick_rows(xp_ref, xs_ref), g_ref[...]).astype(BF16)


def _first_norm(lidx, xp, xs, g_pre):
    return pl.pallas_call(
        _norm_kernel,
        grid_spec=_grid_spec((M_ROWS // NORM_ROWS,),
                             [_prompt_row_spec(), _sample_row_spec(), _gain_spec()], _row_spec()),
        out_shape=jax.ShapeDtypeStruct((M_ROWS, D_MODEL), BF16),
        compiler_params=_params(1, NORM_ROWS * D_MODEL * 28),
        name="first_norm",
    )(lidx, xp, xs, g_pre)


def _resid_norm_body(x, o_ref, gpost_ref, gnext_ref, xn_ref, h_ref):
    xn = x + _rms(o_ref[...], gpost_ref[...])
    xn_ref[...] = xn
    h_ref[...] = _rms(xn, gnext_ref[...]).astype(BF16)


def _resid_norm_kernel(l_ref, x_ref, o_ref, gpost_ref, gnext_ref, xn_ref, h_ref):
    _resid_norm_body(x_ref[...], o_ref, gpost_ref, gnext_ref, xn_ref, h_ref)


def _resid_norm_split_kernel(l_ref, xp_ref, xs_ref, o_ref, gpost_ref, gnext_ref, xn_ref, h_ref):
    _resid_norm_body(_pick_rows(xp_ref, xs_ref), o_ref, gpost_ref, gnext_ref, xn_ref, h_ref)


def _resid_norm(lidx, x, o, g_post, g_next, next_shift):
    if isinstance(x, tuple):
        kern, x_specs = _resid_norm_split_kernel, [_prompt_row_spec(), _sample_row_spec()]
    else:
        kern, x_specs, x = _resid_norm_kernel, [_row_spec()], (x,)
    return pl.pallas_call(
        kern,
        grid_spec=_grid_spec((M_ROWS // NORM_ROWS,),
                             x_specs + [_row_spec(), _gain_spec(), _gain_spec(next_shift)],
                             [_row_spec(), _row_spec()]),
        out_shape=[jax.ShapeDtypeStruct((M_ROWS, D_MODEL), F32),
                   jax.ShapeDtypeStruct((M_ROWS, D_MODEL), BF16)],
        compiler_params=_params(1, NORM_ROWS * D_MODEL * 48),
        name="resid_norm",
    )(lidx, *x, o, g_post, g_next)


def _resid_split_kernel(l_ref, x_ref, o_ref, gpost_ref, yp_ref, ys_ref):
    xn = x_ref[...] + _rms(o_ref[...], gpost_ref[...])
    is_prompt = pl.program_id(0) < N_PROMPT_BLOCKS

    @pl.when(is_prompt)
    def _():
        yp_ref[...] = xn

    @pl.when(jnp.logical_not(is_prompt))
    def _():
        ys_ref[...] = xn


def _resid_split(lidx, x, o, g_post):
    return pl.pallas_call(
        _resid_split_kernel,
        grid_spec=_grid_spec((M_ROWS // NORM_ROWS,), [_row_spec(), _row_spec(), _gain_spec()],
                             [_prompt_row_spec(), _sample_row_spec()]),
        out_shape=[jax.ShapeDtypeStruct((M_PROMPT, D_MODEL), F32),
                   jax.ShapeDtypeStruct((M_SAMPLE, D_MODEL), F32)],
        compiler_params=_params(1, NORM_ROWS * D_MODEL * 40),
        name="resid",
    )(lidx, x, o, g_post)


MM_BM = 2176


def _resident(block_shape, index_map):
    return pl.BlockSpec(block_shape, index_map, pipeline_mode=pl.Buffered(1))


def _mm_kernel(l_ref, x_ref, w_ref, o_ref):
    o_ref[...] = jnp.dot(x_ref[...], w_ref[...].astype(BF16), preferred_element_type=F32)


def _mm_acc_kernel(l_ref, x_ref, w_ref, acc_ref, o_ref):
    o_ref[...] = acc_ref[...] + jnp.dot(x_ref[...], w_ref[...].astype(BF16),
                                        preferred_element_type=F32)


def _matmul(lidx, x, w, *, n_out, bn, k_block=0, k_blocks=1, acc=None, name):
    m = x.shape[0]
    bm = MM_BM
    kt = x.shape[1] // k_blocks
    x_spec = _resident((bm, kt), lambda i, j, l: (i, k_block))
    w_spec = pl.BlockSpec((None, kt, bn), lambda i, j, l: (l[0], k_block, j))
    o_spec = pl.BlockSpec((bm, bn), lambda i, j, l: (i, j))
    if acc is None:
        kern, in_specs, args, aliases = _mm_kernel, [x_spec, w_spec], (lidx, x, w), {}
    else:
        kern, in_specs, args = _mm_acc_kernel, [x_spec, w_spec, o_spec], (lidx, x, w, acc)
        aliases = {3: 0}
    return pl.pallas_call(
        kern,
        grid_spec=_grid_spec((m // bm, n_out // bn), in_specs, o_spec),
        out_shape=jax.ShapeDtypeStruct((m, n_out), F32),
        input_output_aliases=aliases,
        compiler_params=_params(2, VMEM_LIMIT_CAP),
        name=name,
    )(*args)


FFN_BN = 256


def _ffn_in_kernel(l_ref, x_ref, wg_ref, wu_ref, o_ref):
    x = x_ref[...]
    gate = jnp.dot(x, wg_ref[...].astype(BF16), preferred_element_type=F32)
    up = jnp.dot(x, wu_ref[...].astype(BF16), preferred_element_type=F32)
    o_ref[...] = (gate * jax.nn.sigmoid(gate) * up).astype(BF16)


def _ffn_in(lidx, h, w_ffn_in):
    nb = D_FF // FFN_BN
    bm = MM_BM
    x_spec = _resident((bm, D_MODEL), lambda i, j, l: (i, 0))
    wg_spec = pl.BlockSpec((None, D_MODEL, FFN_BN), lambda i, j, l: (l[0], 0, j))
    wu_spec = pl.BlockSpec((None, D_MODEL, FFN_BN), lambda i, j, l: (l[0], 0, j + nb))
    o_spec = pl.BlockSpec((bm, FFN_BN), lambda i, j, l: (i, j))
    return pl.pallas_call(
        _ffn_in_kernel,
        grid_spec=_grid_spec((M_ROWS // bm, nb), [x_spec, wg_spec, wu_spec], o_spec),
        out_shape=jax.ShapeDtypeStruct((M_ROWS, D_FF), BF16),
        compiler_params=_params(2, VMEM_LIMIT_CAP),
        name="ffn_in",
    )(lidx, h, w_ffn_in, w_ffn_in)


GM_BN = 256
GM_ROW_CHUNKS = 4


def _gate_merge_kernel(l_ref, h_ref, y_ref, wg_ref, p_ref, o_ref, acc_ref):
    n = pl.program_id(2)

    @pl.when(n == 0)
    def _():
        acc_ref[...] = jnp.zeros_like(acc_ref)

    wg = wg_ref[...].astype(BF16)
    p = p_ref[...].astype(BF16)
    chunk = acc_ref.shape[0] // GM_ROW_CHUNKS
    for c in range(GM_ROW_CHUNKS):
        rows = slice(c * chunk, (c + 1) * chunk)
        gate = jnp.dot(h_ref[rows, :], wg, preferred_element_type=F32)
        proj = jnp.dot(y_ref[n, rows, :], p, preferred_element_type=F32)
        acc_ref[rows, :] += jax.nn.sigmoid(gate) * proj

    @pl.when(n == pl.num_programs(2) - 1)
    def _():
        o_ref[...] = acc_ref[...].astype(BF16)


def _gate_merge(lidx, h, y, w_in, w_branch):
    bm = MM_BM
    gate0 = N_MIX // GM_BN
    per_gate = D_MODEL // GM_BN
    h_spec = _resident((bm, D_MODEL), lambda i, j, n, l: (i, 0))
    y_spec = _resident((4, bm, D_BR), lambda i, j, n, l: (0, i, 0))
    wg_spec = pl.BlockSpec((None, D_MODEL, GM_BN),
                           lambda i, j, n, l: (l[0], 0, gate0 + n * per_gate + j))
    p_spec = pl.BlockSpec((None, None, D_BR, GM_BN), lambda i, j, n, l: (l[0], n, 0, j))
    o_spec = pl.BlockSpec((bm, GM_BN), lambda i, j, n, l: (i, j))
    vmem = VMEM_LIMIT_CAP
    return pl.pallas_call(
        _gate_merge_kernel,
        grid_spec=_grid_spec((M_ROWS // bm, per_gate, 4), [h_spec, y_spec, wg_spec, p_spec],
                             o_spec, [pltpu.VMEM((bm, GM_BN), F32)]),
        out_shape=jax.ShapeDtypeStruct((M_ROWS, D_MODEL), BF16),
        compiler_params=_params(3, vmem),
        name="gate_merge",
    )(lidx, h, y, w_in, w_branch)


def _gelu(x):
    return 0.5 * x * (1.0 + lax.erf(x * (1.0 / math.sqrt(2.0))))


def _layer_norm(x, g, b):
    mu = jnp.mean(x, axis=-1, keepdims=True)
    xc = x - mu
    var = jnp.mean(xc * xc, axis=-1, keepdims=True)
    return xc * lax.rsqrt(var + LN_EPS) * g + b


def _silu(x):
    return x * jax.nn.sigmoid(x)


MIX_TB = 256
HALO_A, HALO_B, HALO_D = 8, 16, 32


MIX_PROMPT_INPUTS = 24


def _mix_prompt_kernel(l_ref, *refs):
    y_ref = refs[MIX_PROMPT_INPUTS]
    is_fill = pl.program_id(0) == BATCH

    @pl.when(is_fill)
    def _():
        y_ref[...] = jnp.zeros_like(y_ref)

    @pl.when(jnp.logical_not(is_fill))
    def _():
        _mix_prompt_body(l_ref, *refs)


def _mix_prompt_body(l_ref, ab_ref, ac_ref, ah_ref, p_ref, cu_ref, cv_ref, da_ref, dg_ref,
                     ach_ref, ahh_ref, ph_ref, dah_ref, dgh_ref,
                     caw_ref, pw_ref, ps_ref, sng_ref, snb_ref, sw_ref, sbt_ref,
                     cdw_ref, cdb_ref, cng_ref, cnb_ref,
                     y_ref, na_ref, nb_ref, nd_ref,
                     qpad, ppad, dpad, conv_scr, phase_scr):
    t = pl.program_id(1)
    has_past = t > 0
    tb = MIX_TB

    qpad[0:HALO_A, :] = jnp.where(has_past, ach_ref[...] * ahh_ref[...], 0.0)
    qpad[HALO_A:, :] = ac_ref[...] * ah_ref[...]
    conv_a = (caw_ref[0:1, :] * qpad[HALO_A - 2:HALO_A - 2 + tb, :]
              + caw_ref[1:2, :] * qpad[HALO_A - 1:HALO_A - 1 + tb, :]
              + caw_ref[2:3, :] * qpad[HALO_A:HALO_A + tb, :])
    y_ref[0] = (ab_ref[...] * conv_a).astype(BF16)

    ppad[0:HALO_B, :] = jnp.where(has_past, ph_ref[...], 0.0)
    ppad[HALO_B:, :] = p_ref[...]
    pos = t * tb + lax.broadcasted_iota(jnp.int32, (tb, 1), 0)
    for gi, w in enumerate(POOL_WINDOWS):
        cols = slice(gi * POOL_GROUP, (gi + 1) * POOL_GROUP)
        s = ppad[HALO_B:HALO_B + tb, cols]
        for i in range(1, w):
            s = s + ppad[HALO_B - i:HALO_B - i + tb, cols]
        cnt = jnp.minimum(w, pos + 1).astype(F32)
        pooled = s / cnt - ppad[HALO_B:HALO_B + tb, cols]
        yb = jnp.dot(pooled.astype(BF16), pw_ref[gi].astype(BF16), preferred_element_type=F32)
        y_ref[1, :, cols] = (yb * ps_ref[:, cols]).astype(BF16)

    u = _gelu(cu_ref[...])
    v = _layer_norm(_gelu(cv_ref[...]), sng_ref[...], snb_ref[...]).astype(BF16)
    tril = (lax.broadcasted_iota(jnp.int32, (CHUNK, CHUNK), 0)
            >= lax.broadcasted_iota(jnp.int32, (CHUNK, CHUNK), 1))
    for h in range(N_SGU_HEADS):
        w_h = jnp.where(tril, sw_ref[h], 0.0).astype(BF16)
        bias = sbt_ref[:, h:h + 1]
        hc = slice(h * SGU_HEAD, (h + 1) * SGU_HEAD)
        for c in range(tb // CHUNK):
            rows = slice(c * CHUNK, (c + 1) * CHUNK)
            s = jnp.dot(w_h, v[rows, hc], preferred_element_type=F32) + bias
            y_ref[2, rows, hc] = (u[rows, hc] * s).astype(BF16)

    dpad[0:HALO_D, :] = jnp.where(has_past, dah_ref[...] * jax.nn.sigmoid(dgh_ref[...]), 0.0)
    dpad[HALO_D:, :] = da_ref[...] * jax.nn.sigmoid(dg_ref[...])
    base = HALO_D - (CONV_D - 1)
    for c in range(D_BR // LANES):
        cols = slice(c * LANES, (c + 1) * LANES)
        out = None
        for r in range(SUBLANES):
            rows = tb if r == 0 else tb + SUBLANES
            part = None
            for k in range(CONV_D):
                if (base + k) % SUBLANES != r:
                    continue
                row0 = base + k - r
                term = cdw_ref[k:k + 1, cols] * dpad[row0:row0 + rows, cols]
                part = term if part is None else part + term
            if r == 0:
                out = part
            else:
                phase_scr[r - 1] = part
        for r in range(1, SUBLANES):
            out = out + phase_scr[r - 1, r:r + tb, :]
        conv_scr[:, cols] = out
    conv_d = conv_scr[...] + cdb_ref[...]
    y_ref[3] = _silu(_layer_norm(conv_d, cng_ref[...], cnb_ref[...])).astype(BF16)

    @pl.when(t == pl.num_programs(1) - 1)
    def _():
        na_ref[...] = qpad[HALO_A + tb - (CONV_A - 1):HALO_A + tb, :]
        nb_ref[...] = ppad[HALO_B + tb - POOL_BUF:HALO_B + tb, :]
        nd_ref[...] = dpad[HALO_D + tb - (CONV_D - 1):HALO_D + tb, :]


def _layer_param_specs(index_args):
    def spec(shape):
        zeros = (0,) * len(shape)
        if index_args == 2:
            return pl.BlockSpec((None,) + shape, lambda b, t, l: (l[0],) + zeros)
        return pl.BlockSpec((None,) + shape, lambda s, l: (l[0],) + zeros)
    return spec


def _mix_prompt(lidx, z, prm):
    tb = MIX_TB
    n_t = SEQ // tb

    n_prompt = BATCH * n_t
    n_fill = M_SAMPLE // tb

    def in_block(b, t):
        return jnp.minimum(b * n_t + t, n_prompt - 1)

    def y_block(b, t):
        return jnp.where(b < BATCH, b * n_t + t, n_prompt + jnp.minimum(t, n_fill - 1))

    def cur(c):
        return pl.BlockSpec((tb, D_BR), lambda b, t, l: (in_block(b, t), c))

    def halo(c, rows):
        per = tb // rows
        return pl.BlockSpec(
            (rows, D_BR), lambda b, t, l: (jnp.maximum(in_block(b, t) * per - 1, 0), c))

    def state_out(rows):
        return pl.BlockSpec((None, rows, D_BR), lambda b, t, l: (jnp.minimum(b, BATCH - 1), 0, 0))

    spec = _layer_param_specs(2)
    in_specs = ([cur(c) for c in range(8)]
                + [halo(1, HALO_A), halo(2, HALO_A), halo(3, HALO_B),
                   halo(6, HALO_D), halo(7, HALO_D)]
                + [spec((CONV_A, D_BR)), spec((4, POOL_GROUP, POOL_GROUP)), spec((1, D_BR)),
                   spec((1, D_BR)), spec((1, D_BR)), spec((N_SGU_HEADS, CHUNK, CHUNK)),
                   spec((CHUNK, N_SGU_HEADS)), spec((CONV_D, D_BR)), spec((1, D_BR)),
                   spec((1, D_BR)), spec((1, D_BR))])
    out_specs = [pl.BlockSpec((4, tb, D_BR), lambda b, t, l: (0, y_block(b, t), 0)),
                 state_out(CONV_A - 1), state_out(POOL_BUF), state_out(CONV_D - 1)]
    assert len(in_specs) == MIX_PROMPT_INPUTS
    out_shape = [jax.ShapeDtypeStruct((4, M_ROWS, D_BR), BF16),
                 jax.ShapeDtypeStruct((BATCH, CONV_A - 1, D_BR), F32),
                 jax.ShapeDtypeStruct((BATCH, POOL_BUF, D_BR), F32),
                 jax.ShapeDtypeStruct((BATCH, CONV_D - 1, D_BR), F32)]
    scratch = [pltpu.VMEM((HALO_A + tb, D_BR), F32), pltpu.VMEM((HALO_B + tb, D_BR), F32),
               pltpu.VMEM((HALO_D + tb, D_BR), F32), pltpu.VMEM((tb, D_BR), F32),
               pltpu.VMEM((SUBLANES - 1, tb + SUBLANES, LANES), F32)]
    vmem = 2 * 8 * tb * D_BR * 4 + 2 * tb * D_MODEL * 2 + 5 * tb * D_BR * 4 + 24 * tb * D_BR * 4
    return pl.pallas_call(
        _mix_prompt_kernel,
        grid_spec=_grid_spec((BATCH + 1, n_t), in_specs, out_specs, scratch),
        out_shape=out_shape,
        compiler_params=_params(2, vmem),
        name="mix_prompt",
    )(lidx, *([z] * 13), prm["conv_a_w"], prm["pool_w"], prm["pool_scale"], prm["sgu_norm_g"],
      prm["sgu_norm_b"], prm["sgu_w"], prm["sgu_b_t"], prm["conv_d_w"], prm["conv_d_b"],
      prm["conv_norm_g"], prm["conv_norm_b"])


MIX_SS = 16


def _mix_sample_kernel(l_ref, ab_ref, ac_ref, ah_ref, p_ref, cu_ref, cv_ref, da_ref, dg_ref,
                       pa_ref, pb_ref, pd_ref,
                       caw_ref, pw_ref, ps_ref, sng_ref, snb_ref, swc_ref, sbc_ref,
                       cdw_ref, cdb_ref, cng_ref, cnb_ref,
                       y_ref, na_ref, nb_ref, nd_ref, nv_ref):
    ss = MIX_SS

    a_rows = ([pa_ref[r] for r in range(CONV_A - 1)]
              + [ac_ref[t] * ah_ref[t] for t in range(DEC_SEQ)])
    for t in range(DEC_SEQ):
        conv_a = (caw_ref[0:1, :] * a_rows[t] + caw_ref[1:2, :] * a_rows[t + 1]
                  + caw_ref[2:3, :] * a_rows[t + 2])
        y_ref[t, :, 0:D_BR] = (ab_ref[t] * conv_a).astype(BF16)
    for r in range(CONV_A - 1):
        na_ref[r] = a_rows[DEC_SEQ + r]

    def b_row(r, cols):
        if r < POOL_BUF:
            return pb_ref[r, :, cols]
        return p_ref[r - POOL_BUF, :, cols]

    for gi, w in enumerate(POOL_WINDOWS):
        cols = slice(gi * POOL_GROUP, (gi + 1) * POOL_GROUP)
        pooled = []
        for t in range(DEC_SEQ):
            s = b_row(POOL_BUF + t, cols)
            for i in range(1, w):
                s = s + b_row(POOL_BUF + t - i, cols)
            pooled.append(s / float(w) - b_row(POOL_BUF + t, cols))
        yb = jnp.dot(jnp.concatenate(pooled, axis=0).astype(BF16), pw_ref[gi].astype(BF16),
                     preferred_element_type=F32) * ps_ref[:, cols]
        for t in range(DEC_SEQ):
            y_ref[t, :, D_BR + gi * POOL_GROUP:D_BR + (gi + 1) * POOL_GROUP] = (
                yb[t * ss:(t + 1) * ss].astype(BF16))
    for r in range(POOL_BUF - DEC_SEQ):
        nb_ref[r] = pb_ref[r + DEC_SEQ]
    for t in range(DEC_SEQ):
        nb_ref[POOL_BUF - DEC_SEQ + t] = p_ref[t]

    v_rows = [_layer_norm(_gelu(cv_ref[t]), sng_ref[...], snb_ref[...]) for t in range(DEC_SEQ)]
    for t in range(DEC_SEQ):
        nv_ref[t] = v_rows[t]
        s = sbc_ref[t:t + 1, :]
        for j in range(t + 1):
            s = s + swc_ref[t * DEC_SEQ + j:t * DEC_SEQ + j + 1, :] * v_rows[j]
        y_ref[t, :, 2 * D_BR:3 * D_BR] = (_gelu(cu_ref[t]) * s).astype(BF16)

    hist = CONV_D - 1
    for r in range(hist - DEC_SEQ):
        nd_ref[r] = pd_ref[r + DEC_SEQ]
    for t in range(DEC_SEQ):
        nd_ref[hist - DEC_SEQ + t] = da_ref[t] * jax.nn.sigmoid(dg_ref[t])

    def d_row(r, cols):
        if r < hist:
            return pd_ref[r, :, cols]
        return nd_ref[r - DEC_SEQ, :, cols]

    for t in range(DEC_SEQ):
        parts = []
        for c in range(D_BR // LANES):
            cols = slice(c * LANES, (c + 1) * LANES)
            acc = cdb_ref[:, cols] + cdw_ref[0:1, cols] * d_row(t, cols)
            for k in range(1, CONV_D):
                acc = acc + cdw_ref[k:k + 1, cols] * d_row(t + k, cols)
            parts.append(acc)
        conv_d = jnp.concatenate(parts, axis=1)
        y_ref[t, :, 3 * D_BR:] = _silu(
            _layer_norm(conv_d, cng_ref[...], cnb_ref[...])).astype(BF16)


def _mix_sample(lidx, zs, states, prm):
    def cur(c):
        return pl.BlockSpec((DEC_SEQ, MIX_SS, D_BR), lambda s, l: (0, s, c))

    def state(rows):
        return pl.BlockSpec((None, rows, MIX_SS, D_BR), lambda s, l: (l[0], 0, s, 0))

    def new_state(rows):
        return pl.BlockSpec((rows, MIX_SS, D_BR), lambda s, l: (0, s, 0))

    spec = _layer_param_specs(1)
    in_specs = ([cur(c) for c in range(8)]
                + [state(CONV_A - 1), state(POOL_BUF), state(CONV_D - 1)]
                + [spec((CONV_A, D_BR)), spec((4, POOL_GROUP, POOL_GROUP)), spec((1, D_BR)),
                   spec((1, D_BR)), spec((1, D_BR)), spec((DEC_SEQ * DEC_SEQ, D_BR)),
                   spec((DEC_SEQ, D_BR)), spec((CONV_D, D_BR)), spec((1, D_BR)),
                   spec((1, D_BR)), spec((1, D_BR))])
    out_specs = [pl.BlockSpec((DEC_SEQ, MIX_SS, D_MODEL), lambda s, l: (0, s, 0)),
                 new_state(CONV_A - 1), new_state(POOL_BUF), new_state(CONV_D - 1),
                 new_state(DEC_SEQ)]
    out_shape = [jax.ShapeDtypeStruct((DEC_SEQ, DEC_BATCH, D_MODEL), BF16),
                 jax.ShapeDtypeStruct((CONV_A - 1, DEC_BATCH, D_BR), F32),
                 jax.ShapeDtypeStruct((POOL_BUF, DEC_BATCH, D_BR), F32),
                 jax.ShapeDtypeStruct((CONV_D - 1, DEC_BATCH, D_BR), F32),
                 jax.ShapeDtypeStruct((DEC_SEQ, DEC_BATCH, D_BR), F32)]
    state_rows = (CONV_A - 1) + POOL_BUF + (CONV_D - 1) + DEC_SEQ
    tile = MIX_SS * D_BR * 4
    vmem = (2 * 8 * DEC_SEQ * tile + 4 * state_rows * tile + 2 * DEC_SEQ * MIX_SS * D_MODEL * 2
            + 64 * tile)
    return pl.pallas_call(
        _mix_sample_kernel,
        grid_spec=_grid_spec((DEC_BATCH // MIX_SS,), in_specs, out_specs),
        out_shape=out_shape,
        compiler_params=_params(1, vmem),
        name="mix_sample",
    )(lidx, *([zs] * 8), states["a"], states["b"], states["d"],
      prm["conv_a_w"], prm["pool_w"], prm["pool_scale"], prm["sgu_norm_g"], prm["sgu_norm_b"],
      prm["sgu_w_coef"], prm["sgu_b_coef"], prm["conv_d_w"], prm["conv_d_b"],
      prm["conv_norm_g"], prm["conv_norm_b"])


def _row_param(p):
    return p.reshape(DEPTH, 1, p.shape[-1])


def kernel(x_prompt, x_sample, state_conv_a, state_pool_b, state_conv_d, w_in, conv_a_w, pool_w, pool_scale, sgu_norm_g, sgu_norm_b, sgu_w, sgu_b, conv_d_w, conv_d_b, conv_norm_g, conv_norm_b, w_branch, w_o, w_ffn_in, w_ffn_out, norm_mix_pre, norm_mix_post, norm_ffn_pre, norm_ffn_post):
    x = (x_prompt.reshape(M_PROMPT, D_MODEL), x_sample.reshape(M_SAMPLE, D_MODEL))

    sw = sgu_w[:, :, :DEC_SEQ, :DEC_SEQ]
    sgu_w_coef = jnp.repeat(jnp.transpose(sw, (0, 2, 3, 1)), SGU_HEAD, axis=-1)
    sgu_w_coef = sgu_w_coef.reshape(DEPTH, DEC_SEQ * DEC_SEQ, D_BR)
    sgu_b_coef = jnp.repeat(jnp.transpose(sgu_b[:, :, :DEC_SEQ], (0, 2, 1)), SGU_HEAD, axis=-1)

    prm = {
        "conv_a_w": conv_a_w, "pool_w": pool_w, "pool_scale": _row_param(pool_scale),
        "sgu_norm_g": _row_param(sgu_norm_g), "sgu_norm_b": _row_param(sgu_norm_b),
        "sgu_w": sgu_w, "sgu_b_t": jnp.transpose(sgu_b, (0, 2, 1)),
        "sgu_w_coef": sgu_w_coef, "sgu_b_coef": sgu_b_coef,
        "conv_d_w": conv_d_w, "conv_d_b": _row_param(conv_d_b),
        "conv_norm_g": _row_param(conv_norm_g), "conv_norm_b": _row_param(conv_norm_b),
    }
    states = {"a": jnp.transpose(state_conv_a, (0, 2, 1, 3)),
              "b": jnp.transpose(state_pool_b, (0, 2, 1, 3)),
              "d": jnp.transpose(state_conv_d, (0, 2, 1, 3))}
    g_mix_pre, g_mix_post = _row_param(norm_mix_pre), _row_param(norm_mix_post)
    g_ffn_pre, g_ffn_post = _row_param(norm_ffn_pre), _row_param(norm_ffn_post)

    outs = {k: [] for k in ("a_p", "a_s", "b_p", "b_s", "d_p", "d_s", "v_s")}
    h = None
    for layer in range(DEPTH):
        lidx = jnp.full((1,), layer, jnp.int32)
        if layer == 0:
            h = _first_norm(lidx, *x, g_mix_pre)
        z = _matmul(lidx, h, w_in, n_out=N_MIX, bn=512, name="proj_in")
        y, a_p, b_p, d_p = _mix_prompt(lidx, z, prm)
        zs = jnp.transpose(z[M_PROMPT:].reshape(DEC_BATCH, DEC_SEQ, N_MIX), (1, 0, 2))
        ys, a_s, b_s, d_s, v_s = _mix_sample(lidx, zs, states, prm)
        ys = jnp.transpose(ys.reshape(DEC_SEQ, DEC_BATCH, 4, D_BR), (2, 1, 0, 3))
        y = lax.dynamic_update_slice(y, ys.reshape(4, M_SAMPLE, D_BR), (0, M_PROMPT, 0))
        merged = _gate_merge(lidx, h, y, w_in, w_branch)
        mix = _matmul(lidx, merged, w_o, n_out=D_MODEL, bn=512, name="proj_o")
        x, h = _resid_norm(lidx, x, mix, g_mix_post, g_ffn_pre, 0)
        act = _ffn_in(lidx, h, w_ffn_in)
        ffn = _matmul(lidx, act, w_ffn_out, n_out=D_MODEL, bn=256, k_block=0, k_blocks=2,
                      name="ffn_out_lo")
        ffn = _matmul(lidx, act, w_ffn_out, n_out=D_MODEL, bn=256, k_block=1, k_blocks=2,
                      acc=ffn, name="ffn_out_hi")
        if layer + 1 < DEPTH:
            x, h = _resid_norm(lidx, x, ffn, g_ffn_post, g_mix_pre, 1)
        else:
            y_prompt, y_sample = _resid_split(lidx, x, ffn, g_ffn_post)
        for k, val in zip(outs, (a_p, a_s, b_p, b_s, d_p, d_s, v_s)):
            outs[k].append(val)

    y_prompt = y_prompt.reshape(BATCH, SEQ, D_MODEL)
    y_sample = y_sample.reshape(DEC_BATCH, DEC_SEQ, D_MODEL)
    stacked = {k: jnp.stack(v) for k, v in outs.items()}
    for k in ("a_s", "b_s", "d_s", "v_s"):
        stacked[k] = jnp.transpose(stacked[k], (0, 2, 1, 3))
    return (y_prompt, y_sample) + tuple(stacked[k] for k in outs)
```
